```python
import math
import jax, jax.numpy as jnp
from jax import lax
import numpy as np

D_MODEL = 4096
BATCH = 2
SEQ = 4096
DEPTH = 4

MEM_LEN = 256
MIX_WIDTH = D_MODEL
ATTN_WIDTH = MIX_WIDTH // 2
DN_WIDTH = MIX_WIDTH - ATTN_WIDTH
ATTN_HEAD_DIM = 64
ATTN_Q_HEADS = ATTN_WIDTH // ATTN_HEAD_DIM
ATTN_KV_HEADS = 4
ATTN_GROUP = ATTN_Q_HEADS // ATTN_KV_HEADS
ATTN_KV_WIDTH = ATTN_KV_HEADS * ATTN_HEAD_DIM
WINDOW = 128
ATTN_BLOCK = 128
DN_HEAD_DIM = 128
DN_HEADS = DN_WIDTH // DN_HEAD_DIM
CONV_WIDTH = 4
DN_CHUNK = 64
X_HEADS = 4
X_HEAD_DIM = 128
X_WIDTH = X_HEADS * X_HEAD_DIM
FFN_HIDDEN = -(-8 * D_MODEL // (3 * 256)) * 256
RMS_EPS = 1e-6

OFF_AQ = 0
OFF_AK = OFF_AQ + ATTN_WIDTH
OFF_AV = OFF_AK + ATTN_KV_WIDTH
OFF_DQKV = OFF_AV + ATTN_KV_WIDTH
OFF_DZ = OFF_DQKV + 3 * DN_WIDTH
OFF_DA = OFF_DZ + DN_WIDTH
OFF_DB = OFF_DA + DN_HEADS
IN_COLS = OFF_DB + DN_HEADS

kernel_name = "hymba_swa_sink_gdn_hybrid"


def rmsnorm(x, w):
    xf = x.astype(jnp.float32)
    y = xf * lax.rsqrt(jnp.mean(xf * xf, axis=-1, keepdims=True) + RMS_EPS)
    return (y * w.astype(jnp.float32)).astype(x.dtype)


def l2norm(x):
    return x * lax.rsqrt(jnp.sum(x * x, axis=-1, keepdims=True) + 1e-6)


def alibi_slopes():
    h = jnp.arange(1, ATTN_Q_HEADS + 1, dtype=jnp.float32)
    return jnp.exp2(-8.0 * h / ATTN_Q_HEADS).reshape(ATTN_KV_HEADS, ATTN_GROUP)


def sliding_window_attention(q, k, v, sinks):
    B, S = q.shape[:2]
    L = ATTN_BLOCK
    nb = S // L
    qb = q.reshape(B, nb, L, ATTN_KV_HEADS, ATTN_GROUP, ATTN_HEAD_DIM)

    def with_prev(t):
        tb = t.reshape(B, nb, L, ATTN_KV_HEADS, ATTN_HEAD_DIM)
        prev = jnp.pad(tb[:, :-1], ((0, 0), (1, 0), (0, 0), (0, 0), (0, 0)))
        return jnp.concatenate([prev, tb], axis=2)

    kb, vb = with_prev(k), with_prev(v)
    scores = jnp.einsum('bnqhgd,bnkhd->bnhgqk', qb, kb).astype(jnp.float32) * (ATTN_HEAD_DIM ** -0.5)
    r = jnp.arange(L)[:, None]
    c = jnp.arange(2 * L)[None, :]
    dist = r + L - c
    in_win = (dist >= 0) & (dist < WINDOW)
    valid = in_win[None] & ((jnp.arange(nb)[:, None, None] > 0) | (c >= L)[None])
    slopes = alibi_slopes()
    scores = scores - slopes[:, :, None, None] * dist.astype(jnp.float32)
    scores = jnp.where(valid[None, :, None, None], scores, -jnp.inf)
    sink = sinks.astype(jnp.float32).reshape(ATTN_KV_HEADS, ATTN_GROUP)[None, None, :, :, None, None]
    m = jnp.maximum(jnp.max(scores, axis=-1, keepdims=True), sink)
    p = jnp.exp(scores - m)
    denom = jnp.sum(p, axis=-1, keepdims=True) + jnp.exp(sink - m)
    p = (p / denom).astype(v.dtype)
    out = jnp.einsum('bnhgqk,bnkhd->bnqhgd', p, vb)
    return out.reshape(B, S, ATTN_WIDTH)


def causal_depthwise_conv(x, w):
    C = x.shape[-1]
    return lax.conv_general_dilated(
        x, w[:, None, :].astype(x.dtype), window_strides=(1,), padding=[(CONV_WIDTH - 1, 0)],
        dimension_numbers=('NWC', 'WIO', 'NWC'), feature_group_count=C)


def gated_delta_rule(q, k, v, g, beta):
    B, S, H, D = q.shape
    C = DN_CHUNK
    nc = S // C
    q = l2norm(q) * (D ** -0.5)
    k = l2norm(k)

    def chunks(t):
        return t.reshape(B, nc, C, H, -1).transpose(0, 3, 1, 2, 4)

    qc, kc, vc = chunks(q), chunks(k), chunks(v)
    gcum = jnp.cumsum(g.reshape(B, nc, C, H).transpose(0, 3, 1, 2), axis=-1)
    bc = beta.reshape(B, nc, C, H).transpose(0, 3, 1, 2)
    causal = jnp.tril(jnp.ones((C, C), dtype=bool))
    strict = jnp.tril(jnp.ones((C, C), dtype=bool), k=-1)
    diff = gcum[..., :, None] - gcum[..., None, :]
    decay = jnp.exp(jnp.where(causal, diff, -jnp.inf))
    kb = kc * bc[..., None]
    low = jnp.where(strict, jnp.einsum('bhncd,bhnsd->bhncs', kb, kc) * decay, 0.0)
    rhs = jnp.concatenate([vc * bc[..., None], kb * jnp.exp(gcum)[..., None]], axis=-1)
    sol = lax.linalg.triangular_solve(low, rhs, left_side=True, lower=True, unit_diagonal=True)
    u, w = sol[..., :D], sol[..., D:]
    a_intra = jnp.where(causal, jnp.einsum('bhncd,bhnsd->bhncs', qc, kc) * decay, 0.0)
    q_dec = qc * jnp.exp(gcum)[..., None]
    k_dec = kc * jnp.exp(gcum[..., -1:] - gcum)[..., None]
    g_last = jnp.exp(gcum[..., -1])

    def step(state, xs):
        u_i, w_i, qd_i, kd_i, a_i, gl_i = xs
        v_new = u_i - jnp.einsum('bhcd,bhde->bhce', w_i, state)
        o = jnp.einsum('bhcd,bhde->bhce', qd_i, state) + jnp.einsum('bhcs,bhse->bhce', a_i, v_new)
        state = state * gl_i[..., None, None] + jnp.einsum('bhcd,bhce->bhde', kd_i, v_new)
        return state, o

    mv = lambda t: jnp.moveaxis(t, 2, 0)
    xs = (mv(u), mv(w), mv(q_dec), mv(k_dec), mv(a_intra), mv(g_last))
    s0 = jnp.zeros((B, H, D, v.shape[-1]), jnp.float32)
    _, o = lax.scan(step, s0, xs)
    return o.transpose(1, 0, 3, 2, 4).reshape(B, S, H, -1)


def memory_cross_attention(h, mem_n, w_q, w_kv, w_o):
    B, S, _ = h.shape
    q = (h @ w_q).reshape(B, S, X_HEADS, X_HEAD_DIM)
    kv = mem_n @ w_kv
    k = kv[..., :X_WIDTH].reshape(B, MEM_LEN, X_HEADS, X_HEAD_DIM)
    v = kv[..., X_WIDTH:].reshape(B, MEM_LEN, X_HEADS, X_HEAD_DIM)
    s = jnp.einsum('bshd,bmhd->bhsm', q, k).astype(jnp.float32) * (X_HEAD_DIM ** -0.5)
    p = jax.nn.softmax(s, axis=-1).astype(v.dtype)
    o = jnp.einsum('bhsm,bmhd->bshd', p, v).reshape(B, S, X_WIDTH)
    return o @ w_o


def setup_inputs(seed: int = 0) -> dict:
    key = jax.random.key(seed)
    ks = jax.random.split(key, 24)
    f32 = jnp.float32

    def normal(k, shape, fan_in):
        return jax.random.normal(k, shape, f32) * (fan_in ** -0.5)

    def gain(k, shape):
        return 1.0 + 0.02 * jax.random.normal(k, shape, f32)

    dt = jnp.exp(jax.random.uniform(ks[8], (DEPTH, DN_HEADS), f32, math.log(1e-3), math.log(1e-1)))
    return {
        "x": jax.random.normal(ks[0], (BATCH, SEQ, D_MODEL), f32),
        "mem": jax.random.normal(ks[1], (BATCH, MEM_LEN, D_MODEL), f32),
        "norm_mix_pre": gain(ks[2], (DEPTH, D_MODEL)),
        "w_in": normal(ks[3], (DEPTH, D_MODEL, IN_COLS), D_MODEL),
        "attn_sinks": 0.5 * jax.random.normal(ks[4], (DEPTH, ATTN_Q_HEADS), f32),
        "dn_conv_w": normal(ks[5], (DEPTH, CONV_WIDTH, 3 * DN_WIDTH), CONV_WIDTH),
        "dn_a_log": jnp.log(jax.random.uniform(ks[6], (DEPTH, DN_HEADS), f32, 1.0, 16.0)),
        "dn_dt_bias": dt + jnp.log(-jnp.expm1(-dt)),
        "dn_out_norm": gain(ks[7], (DEPTH, DN_HEAD_DIM)),
        "w_out": normal(ks[9], (DEPTH, MIX_WIDTH, D_MODEL), MIX_WIDTH),
        "norm_mix_post": gain(ks[10], (DEPTH, D_MODEL)),
        "norm_mem": gain(ks[11], (DEPTH, D_MODEL)),
        "norm_x_pre": gain(ks[12], (DEPTH, D_MODEL)),
        "w_xq": normal(ks[13], (DEPTH, D_MODEL, X_WIDTH), D_MODEL),
        "w_xkv": normal(ks[14], (DEPTH, D_MODEL, 2 * X_WIDTH), D_MODEL),
        "w_xo": normal(ks[15], (DEPTH, X_WIDTH, D_MODEL), X_WIDTH),
        "norm_x_post": gain(ks[16], (DEPTH, D_MODEL)),
        "norm_ffn_pre": gain(ks[17], (DEPTH, D_MODEL)),
        "w_gate_up": normal(ks[18], (DEPTH, D_MODEL, 2 * FFN_HIDDEN), D_MODEL),
        "w_down": normal(ks[19], (DEPTH, FFN_HIDDEN, D_MODEL), FFN_HIDDEN),
        "norm_ffn_post": gain(ks[20], (DEPTH, D_MODEL)),
    }


def reference(x, mem, norm_mix_pre, w_in, attn_sinks, dn_conv_w, dn_a_log, dn_dt_bias, dn_out_norm,
              w_out, norm_mix_post, norm_mem, norm_x_pre, w_xq, w_xkv, w_xo, norm_x_post,
              norm_ffn_pre, w_gate_up, w_down, norm_ffn_post):
    B, S, _ = x.shape
    f32 = jnp.float32
    h = x
    for l in range(DEPTH):
        a = rmsnorm(h, norm_mix_pre[l])
        u = a @ w_in[l]
        qa = u[..., OFF_AQ:OFF_AK].reshape(B, S, ATTN_Q_HEADS, ATTN_HEAD_DIM)
        ka = u[..., OFF_AK:OFF_AV].reshape(B, S, ATTN_KV_HEADS, ATTN_HEAD_DIM)
        va = u[..., OFF_AV:OFF_DQKV].reshape(B, S, ATTN_KV_HEADS, ATTN_HEAD_DIM)
        attn_out = sliding_window_attention(qa, ka, va, attn_sinks[l])

        qkv = jax.nn.silu(causal_depthwise_conv(u[..., OFF_DQKV:OFF_DZ], dn_conv_w[l])).astype(f32)
        qd = qkv[..., :DN_WIDTH].reshape(B, S, DN_HEADS, DN_HEAD_DIM)
        kd = qkv[..., DN_WIDTH:2 * DN_WIDTH].reshape(B, S, DN_HEADS, DN_HEAD_DIM)
        vd = qkv[..., 2 * DN_WIDTH:].reshape(B, S, DN_HEADS, DN_HEAD_DIM)
        z = u[..., OFF_DZ:OFF_DA].astype(f32).reshape(B, S, DN_HEADS, DN_HEAD_DIM)
        g = -jnp.exp(dn_a_log[l].astype(f32)) * jax.nn.softplus(
            u[..., OFF_DA:OFF_DB].astype(f32) + dn_dt_bias[l].astype(f32))
        beta = jax.nn.sigmoid(u[..., OFF_DB:IN_COLS].astype(f32))
        od = gated_delta_rule(qd, kd, vd, g, beta)
        od = rmsnorm(od, dn_out_norm[l]) * jax.nn.silu(z)
        dn_out = od.reshape(B, S, DN_WIDTH).astype(h.dtype)

        mix = jnp.concatenate([attn_out, dn_out], axis=-1) @ w_out[l]
        h = h + rmsnorm(mix, norm_mix_post[l])

        mem_n = rmsnorm(mem, norm_mem[l])
        xo = memory_cross_attention(rmsnorm(h, norm_x_pre[l]), mem_n, w_xq[l], w_xkv[l], w_xo[l])
        h = h + rmsnorm(xo, norm_x_post[l])

        gu = rmsnorm(h, norm_ffn_pre[l]) @ w_gate_up[l]
        ff = (jax.nn.silu(gu[..., :FFN_HIDDEN]) * gu[..., FFN_HIDDEN:]) @ w_down[l]
        h = h + rmsnorm(ff, norm_ffn_post[l])
    return h
```

```python
import functools

import jax
import jax.numpy as jnp
from jax import lax
from jax.experimental import pallas as pl
from jax.experimental.pallas import tpu as pltpu

F32 = jnp.float32
BF16 = jnp.bfloat16

D_MODEL = 4096
DEPTH = 4
MEM_LEN = 256
ATTN_WIDTH = 2048
DN_WIDTH = 2048
ATTN_HEAD_DIM = 64
ATTN_Q_HEADS = 32
ATTN_KV_HEADS = 4
ATTN_GROUP = 8
ATTN_KV_WIDTH = 256
WINDOW = 128
ATTN_BLOCK = 128
DN_HEAD_DIM = 128
DN_HEADS = 16
CONV_WIDTH = 4
DN_CHUNK = 64
X_HEADS = 4
X_HEAD_DIM = 128
X_WIDTH = 512
FFN_HIDDEN = 11008
RMS_EPS = 1e-6

OFF_AQ = 0
OFF_AK = OFF_AQ + ATTN_WIDTH
OFF_AV = OFF_AK + ATTN_KV_WIDTH
OFF_DQKV = OFF_AV + ATTN_KV_WIDTH
OFF_DZ = OFF_DQKV + 3 * DN_WIDTH
OFF_DA = OFF_DZ + DN_WIDTH
IN_MAIN_COLS = OFF_DA
N_GATE_COLS = 2 * DN_HEADS

LANES = 128
SUBLANES = 8
VMEM_LIMIT_BYTES = 56 * 1024 * 1024


def _params(semantics, vmem=VMEM_LIMIT_BYTES):
    return pltpu.CompilerParams(dimension_semantics=semantics, vmem_limit_bytes=vmem)


def _rms(x, gain):
    ms = jnp.mean(x * x, axis=-1, keepdims=True)
    return x * lax.rsqrt(ms + RMS_EPS) * gain


def _norm_kernel(x_ref, g_ref, o_ref):
    o_ref[...] = _rms(x_ref[...], g_ref[...]).astype(o_ref.dtype)


def _norm_cast(x, gains, layer, *, rows):
    m, d = x.shape
    rows = min(rows, m)
    return pl.pallas_call(
        _norm_kernel,
        grid=(m // rows,),
        in_specs=[pl.BlockSpec((rows, d), lambda i: (i, 0)),
                  pl.BlockSpec((None, 1, d), lambda i: (layer, 0, 0))],
        out_specs=pl.BlockSpec((rows, d), lambda i: (i, 0)),
        out_shape=jax.ShapeDtypeStruct((m, d), BF16),
        compiler_params=_params(("parallel",)),
        name="norm_cast",
    )(x, gains)


def _resnorm_kernel(y_ref, h_ref, gp_ref, gn_ref, hn_ref, a_ref):
    hn = h_ref[...] + _rms(y_ref[...], gp_ref[...])
    hn_ref[...] = hn
    a_ref[...] = _rms(hn, gn_ref[...]).astype(a_ref.dtype)


def _resnorm_last_kernel(y_ref, h_ref, gp_ref, hn_ref):
    hn_ref[...] = h_ref[...] + _rms(y_ref[...], gp_ref[...])


def _resnorm(y, h, g_post, l_post, g_next, l_next, *, rows=128):
    m, d = y.shape
    row_spec = pl.BlockSpec((rows, d), lambda i: (i, 0))
    if g_next is None:
        return pl.pallas_call(
            _resnorm_last_kernel,
            grid=(m // rows,),
            in_specs=[row_spec, row_spec, pl.BlockSpec((None, 1, d), lambda i: (l_post, 0, 0))],
            out_specs=row_spec,
            out_shape=jax.ShapeDtypeStruct((m, d), F32),
            compiler_params=_params(("parallel",)),
            name="resnorm_last",
        )(y, h, g_post), None
    return pl.pallas_call(
        _resnorm_kernel,
        grid=(m // rows,),
        in_specs=[row_spec, row_spec,
                  pl.BlockSpec((None, 1, d), lambda i: (l_post, 0, 0)),
                  pl.BlockSpec((None, 1, d), lambda i: (l_next, 0, 0))],
        out_specs=[row_spec, row_spec],
        out_shape=[jax.ShapeDtypeStruct((m, d), F32), jax.ShapeDtypeStruct((m, d), BF16)],
        compiler_params=_params(("parallel",)),
        name="resnorm",
    )(y, h, g_post, g_next)


def _row_resident_spec(tm, k):
    return pl.BlockSpec((tm, k), lambda i, j: (i, 0), pipeline_mode=pl.Buffered(1))


def _mm_kernel(x_ref, w_ref, o_ref, *, n_valid):
    acc = jnp.dot(x_ref[...], w_ref[...].astype(BF16), preferred_element_type=F32)
    if n_valid is not None:
        lane = lax.broadcasted_iota(jnp.int32, acc.shape, 1)
        acc = jnp.where(lane < n_valid, acc, 0.0)
    o_ref[...] = acc.astype(o_ref.dtype)


def _matmul(x, w, layer, *, col_off_blocks, n_cols, tm, tn, out_dtype, n_valid=None, name):
    m, k = x.shape
    tm = min(tm, m)
    assert w.shape[1] == k and m % tm == 0 and n_cols % tn == 0
    return pl.pallas_call(
        functools.partial(_mm_kernel, n_valid=n_valid),
        grid=(m // tm, n_cols // tn),
        in_specs=[_row_resident_spec(tm, k),
                  pl.BlockSpec((None, k, tn), lambda i, j: (layer, 0, j + col_off_blocks))],
        out_specs=pl.BlockSpec((tm, tn), lambda i, j: (i, j)),
        out_shape=jax.ShapeDtypeStruct((m, n_cols), out_dtype),
        compiler_params=_params(("parallel", "arbitrary")),
        name=name,
    )(x, w)


def _swiglu_kernel(x_ref, wg_ref, wu_ref, o_ref):
    x = x_ref[...]
    g = jnp.dot(x, wg_ref[...].astype(BF16), preferred_element_type=F32)
    u = jnp.dot(x, wu_ref[...].astype(BF16), preferred_element_type=F32)
    o_ref[...] = (g * jax.nn.sigmoid(g) * u).astype(o_ref.dtype)


def _matmul_swiglu(x, w, layer, *, tm, tn):
    m, k = x.shape
    tm = min(tm, m)
    hid = w.shape[2] // 2
    assert hid % tn == 0 and m % tm == 0
    up_off = hid // tn
    return pl.pallas_call(
        _swiglu_kernel,
        grid=(m // tm, hid // tn),
        in_specs=[_row_resident_spec(tm, k),
                  pl.BlockSpec((None, k, tn), lambda i, j: (layer, 0, j)),
                  pl.BlockSpec((None, k, tn), lambda i, j: (layer, 0, j + up_off))],
        out_specs=pl.BlockSpec((tm, tn), lambda i, j: (i, j)),
        out_shape=jax.ShapeDtypeStruct((m, hid), BF16),
        compiler_params=_params(("parallel", "arbitrary")),
        name="ffn_gate_up",
    )(x, w, w)


def _alibi_slope(head):
    return 2.0 ** (-8.0 * (head + 1) / ATTN_Q_HEADS)


def _block_diag_pair(blk, use_right):
    lane = lax.broadcasted_iota(jnp.int32, blk.shape, 1)
    left = lane < ATTN_HEAD_DIM
    rolled = pltpu.roll(blk, ATTN_HEAD_DIM, 1)
    if use_right:
        top, bot = jnp.where(left, rolled, 0.0), jnp.where(left, 0.0, blk)
    else:
        top, bot = jnp.where(left, blk, 0.0), jnp.where(left, 0.0, rolled)
    return jnp.concatenate([top, bot], axis=0)


def _swa_kernel(sink_ref, q_ref, kp_ref, kc_ref, vp_ref, vc_ref, o_ref, *, layer):
    n = pl.program_id(1)
    blk = ATTN_BLOCK
    kw = jnp.concatenate([kp_ref[...], kc_ref[...]], axis=0)
    vw = jnp.concatenate([vp_ref[...], vc_ref[...]], axis=0)
    r = lax.broadcasted_iota(jnp.int32, (blk, 2 * blk), 0)
    c = lax.broadcasted_iota(jnp.int32, (blk, 2 * blk), 1)
    dist = r + blk - c
    valid = (dist >= 0) & (dist < WINDOW) & ((n > 0) | (c >= blk))
    distf = dist.astype(F32)
    out_lane = lax.broadcasted_iota(jnp.int32, (blk, LANES), 1)
    for kv_pair in range(ATTN_KV_HEADS // 2):
        kblk = kw[:, kv_pair * LANES:(kv_pair + 1) * LANES]
        vblk = vw[:, kv_pair * LANES:(kv_pair + 1) * LANES]
        for e in range(2):
            kv_head = 2 * kv_pair + e
            kext = _block_diag_pair(kblk, e == 1).astype(BF16)
            vext = _block_diag_pair(vblk, e == 1).astype(BF16)
            for qp in range(ATTN_GROUP // 2):
                head_a = kv_head * ATTN_GROUP + 2 * qp
                cols = slice(head_a * ATTN_HEAD_DIM, (head_a + 2) * ATTN_HEAD_DIM)
                q2 = (q_ref[:, cols] * (ATTN_HEAD_DIM ** -0.5)).astype(BF16)
                s2 = lax.dot_general(q2, kext, (((1,), (1,)), ((), ())),
                                     preferred_element_type=F32)
                probs, inv_den = [], []
                for t in range(2):
                    head = head_a + t
                    s = s2[:, t * 2 * blk:(t + 1) * 2 * blk] - _alibi_slope(head) * distf
                    s = jnp.where(valid, s, -jnp.inf)
                    sink = sink_ref[layer, head]
                    mx = jnp.maximum(jnp.max(s, axis=-1, keepdims=True), sink)
                    p = jnp.exp(s - mx)
                    den = jnp.sum(p, axis=-1, keepdims=True) + jnp.exp(sink - mx)
                    probs.append(p)
                    inv_den.append(1.0 / den)
                p2 = jnp.concatenate(probs, axis=1).astype(BF16)
                o2 = jnp.dot(p2, vext, preferred_element_type=F32)
                scale = jnp.where(out_lane < ATTN_HEAD_DIM, inv_den[0], inv_den[1])
                o_ref[:, cols] = (o2 * scale).astype(o_ref.dtype)


def _swa(u, sinks, layer, *, batch, seq):
    nb = seq // ATTN_BLOCK
    kcol = OFF_AK // ATTN_KV_WIDTH
    vcol = OFF_AV // ATTN_KV_WIDTH

    def cur(col):
        return lambda b, n: (b * nb + n, col)

    def prev(col):
        return lambda b, n: (b * nb + jnp.maximum(n - 1, 0), col)

    kv_shape = (ATTN_BLOCK, ATTN_KV_WIDTH)
    return pl.pallas_call(
        functools.partial(_swa_kernel, layer=layer),
        grid=(batch, nb),
        in_specs=[pl.BlockSpec(memory_space=pltpu.SMEM),
                  pl.BlockSpec((ATTN_BLOCK, ATTN_WIDTH), lambda b, n: (b * nb + n, 0)),
                  pl.BlockSpec(kv_shape, prev(kcol)), pl.BlockSpec(kv_shape, cur(kcol)),
                  pl.BlockSpec(kv_shape, prev(vcol)), pl.BlockSpec(kv_shape, cur(vcol))],
        out_specs=pl.BlockSpec((ATTN_BLOCK, ATTN_WIDTH), lambda b, n: (b * nb + n, 0)),
        out_shape=jax.ShapeDtypeStruct((batch * seq, ATTN_WIDTH), BF16),
        compiler_params=_params(("parallel", "parallel")),
        name="swa",
    )(sinks, u, u, u, u, u)


def _dn_gates_kernel(g_ref, alog_ref, bias_ref, gb_ref, gt_ref):
    raw = g_ref[...]
    rows = raw.shape[0]
    lane = lax.broadcasted_iota(jnp.int32, raw.shape, 1)
    g = -jnp.exp(alog_ref[...]) * jax.nn.softplus(raw + bias_ref[...])
    pos = lax.broadcasted_iota(jnp.int32, raw.shape, 0) % DN_CHUNK
    acc = g
    shift = 1
    while shift < DN_CHUNK:
        acc = acc + jnp.where(pos >= shift, pltpu.roll(acc, shift, 0), 0.0)
        shift *= 2
    gb = jnp.where(lane < DN_HEADS, acc, jnp.where(lane < N_GATE_COLS, jax.nn.sigmoid(raw), 0.0))
    gb_ref[...] = gb
    gt_ref[...] = jnp.swapaxes(gb.reshape(rows // DN_CHUNK, DN_CHUNK, LANES), 1, 2)


def _dn_gates(gates, alog, bias, layer, *, rows=512):
    m = gates.shape[0]
    rows = min(rows, m)
    nc = rows // DN_CHUNK
    return pl.pallas_call(
        _dn_gates_kernel,
        grid=(m // rows,),
        in_specs=[pl.BlockSpec((rows, LANES), lambda i: (i, 0)),
                  pl.BlockSpec((None, 1, LANES), lambda i: (layer, 0, 0)),
                  pl.BlockSpec((None, 1, LANES), lambda i: (layer, 0, 0))],
        out_specs=[pl.BlockSpec((rows, LANES), lambda i: (i, 0)),
                   pl.BlockSpec((nc, LANES, DN_CHUNK), lambda i: (i, 0, 0))],
        out_shape=[jax.ShapeDtypeStruct((m, LANES), F32),
                   jax.ShapeDtypeStruct((m // DN_CHUNK, LANES, DN_CHUNK), F32)],
        compiler_params=_params(("parallel",)),
        name="dn_gates",
    )(gates, alog, bias)


def _bdot(eq, a, b):
    return jnp.einsum(eq, a.astype(BF16), b.astype(BF16), preferred_element_type=F32)


def _split_bf16(x):
    hi = x.astype(BF16)
    return hi, (x - hi.astype(F32)).astype(BF16)


def _bdot3(eq, a, b):
    ah, al = _split_bf16(a)
    bh, bl = _split_bf16(b)
    mm = functools.partial(jnp.einsum, eq, preferred_element_type=F32)
    return mm(ah, bh) + (mm(al, bh) + mm(ah, bl))


def _unit_lower_inverse(low, ri, ci):
    base = SUBLANES
    same = lambda b: (ri & -b) == (ci & -b)
    x = -jnp.where(same(base), low, 0.0)
    inv = (ri == ci).astype(F32) + x
    size = 2
    while size < base:
        x = _bdot3('cij,cjk->cik', x, x)
        inv = inv + _bdot3('cij,cjk->cik', inv, x)
        size *= 2
    blk = base
    while blk < DN_CHUNK:
        off = jnp.where(same(2 * blk) & ~same(blk), low, 0.0)
        inv = inv - _bdot3('cij,cjk->cik', inv, _bdot3('cij,cjk->cik', off, inv))
        blk *= 2
    return inv


def _dn_kernel(xq_ref, xk_ref, xv_ref, z_ref, gb_ref, gt_ref, cwq_ref, cwk_ref, cwv_ref, onorm_ref,
               o_ref, state_ref, tail_ref, *, heads_per_step, rows):
    hg = pl.program_id(1)
    t = pl.program_id(2)
    nc = rows // DN_CHUNK
    dh = DN_HEAD_DIM

    @pl.when(t == 0)
    def _():
        state_ref[...] = jnp.zeros_like(state_ref)
        tail_ref[...] = jnp.zeros_like(tail_ref)

    def conv_silu(x_ref, cw_ref, idx):
        x = x_ref[...]
        ext = jnp.concatenate([tail_ref[idx], x], axis=0)
        w = cw_ref[...]
        y = x * w[CONV_WIDTH - 1:CONV_WIDTH]
        for s in range(1, CONV_WIDTH):
            y = y + pltpu.roll(ext, s, 0)[SUBLANES:] * w[CONV_WIDTH - 1 - s:CONV_WIDTH - s]
        tail_ref[idx] = x[rows - SUBLANES:]
        return y * jax.nn.sigmoid(y)

    yq = conv_silu(xq_ref, cwq_ref, 0)
    yk = conv_silu(xk_ref, cwk_ref, 1)
    yv = conv_silu(xv_ref, cwv_ref, 2)

    gball = gb_ref[...]
    lane = lax.broadcasted_iota(jnp.int32, gball.shape, 1)
    ri = lax.broadcasted_iota(jnp.int32, (DN_CHUNK, DN_CHUNK), 0)
    ci = lax.broadcasted_iota(jnp.int32, (DN_CHUNK, DN_CHUNK), 1)
    causal = ri >= ci
    strict = ri > ci

    neg_p_q, n_mat, o_loc, g_last = [], [], [], []
    for hh in range(heads_per_step):
        head = hg * heads_per_step + hh
        cols = slice(hh * dh, (hh + 1) * dh)
        q = yq[:, cols]
        k = yk[:, cols]
        v = yv[:, cols]
        q = q * lax.rsqrt(jnp.sum(q * q, axis=-1, keepdims=True) + 1e-6) * (dh ** -0.5)
        k = k * lax.rsqrt(jnp.sum(k * k, axis=-1, keepdims=True) + 1e-6)
        gcum = jnp.sum(jnp.where(lane == head, gball, 0.0), axis=-1, keepdims=True)
        beta = jnp.sum(jnp.where(lane == head + DN_HEADS, gball, 0.0), axis=-1, keepdims=True)
        grow = gt_ref[:, pl.ds(head, 1), :]
        q3 = q.reshape(nc, DN_CHUNK, dh)
        k3 = k.reshape(nc, DN_CHUNK, dh)
        v3 = v.reshape(nc, DN_CHUNK, dh)
        gc3 = gcum.reshape(nc, DN_CHUNK, 1)
        b3 = beta.reshape(nc, DN_CHUNK, 1)
        decay = jnp.exp(jnp.where(causal, gc3 - grow, -jnp.inf))
        eg = jnp.exp(gc3)
        gl = gc3[:, DN_CHUNK - 1:DN_CHUNK, :]
        kb = k3 * b3
        kk = _bdot3('cid,cjd->cij', kb, k3)
        qk = _bdot('cid,cjd->cij', q3, k3)
        low = jnp.where(strict, kk * decay, 0.0)
        a_intra = qk * decay
        tinv = _unit_lower_inverse(low, ri, ci)
        rhs = jnp.concatenate([kb * eg, v3 * b3], axis=-1)
        wu = _bdot3('cij,cjd->cid', tinv, rhs)
        kd = k3 * jnp.exp(gl - gc3)
        pn = _bdot('ncd,nce->nde', kd, wu)
        aq = _bdot('cij,cjd->cid', a_intra, wu)
        qeff = q3 * eg - aq[..., :dh]
        neg_p_q.append(jnp.concatenate([-pn[..., :dh], qeff], axis=1))
        n_mat.append(pn[..., dh:])
        o_loc.append(aq[..., dh:])
        g_last.append(jnp.exp(gl))

    states = [state_ref[hh] for hh in range(heads_per_step)]
    outs = [[] for _ in range(heads_per_step)]
    for cidx in range(nc):
        for hh in range(heads_per_step):
            s = states[hh]
            r = jnp.dot(neg_p_q[hh][cidx].astype(BF16), s.astype(BF16), preferred_element_type=F32)
            outs[hh].append(r[dh:] + o_loc[hh][cidx])
            states[hh] = g_last[hh][cidx] * s + r[:dh] + n_mat[hh][cidx]
    for hh in range(heads_per_step):
        state_ref[hh] = states[hh]
        cols = slice(hh * dh, (hh + 1) * dh)
        o = jnp.concatenate(outs[hh], axis=0)
        z = z_ref[:, cols]
        o_ref[:, cols] = (_rms(o, onorm_ref[...]) * (z * jax.nn.sigmoid(z))).astype(o_ref.dtype)


def _deltanet(u, gb, gt, conv_w, out_norm, layer, *, batch, seq, heads_per_step=2, rows=512):
    wcols = heads_per_step * DN_HEAD_DIM
    rows = min(rows, seq)
    nt = seq // rows
    nc = rows // DN_CHUNK
    n_hg = DN_HEADS // heads_per_step
    qoff = OFF_DQKV // wcols
    koff = (OFF_DQKV + DN_WIDTH) // wcols
    voff = (OFF_DQKV + 2 * DN_WIDTH) // wcols
    zoff = OFF_DZ // wcols

    def xspec(off):
        return pl.BlockSpec((rows, wcols), lambda b, g, t: (b * nt + t, off + g))

    def cwspec(off):
        return pl.BlockSpec((None, CONV_WIDTH, wcols), lambda b, g, t: (layer, 0, off + g))

    return pl.pallas_call(
        functools.partial(_dn_kernel, heads_per_step=heads_per_step, rows=rows),
        grid=(batch, n_hg, nt),
        in_specs=[xspec(qoff), xspec(koff), xspec(voff), xspec(zoff),
                  pl.BlockSpec((rows, LANES), lambda b, g, t: (b * nt + t, 0)),
                  pl.BlockSpec((nc, LANES, DN_CHUNK), lambda b, g, t: (b * nt + t, 0, 0)),
                  cwspec(0), cwspec(DN_WIDTH // wcols), cwspec(2 * DN_WIDTH // wcols),
                  pl.BlockSpec((None, 1, DN_HEAD_DIM), lambda b, g, t: (layer, 0, 0))],
        out_specs=pl.BlockSpec((rows, wcols), lambda b, g, t: (b * nt + t, g)),
        out_shape=jax.ShapeDtypeStruct((batch * seq, DN_WIDTH), BF16),
        scratch_shapes=[pltpu.VMEM((heads_per_step, DN_HEAD_DIM, DN_HEAD_DIM), F32),
                        pltpu.VMEM((3, SUBLANES, wcols), F32)],
        compiler_params=_params(("parallel", "parallel", "arbitrary")),
        name="deltanet",
    )(u, u, u, u, gb, gt, conv_w, conv_w, conv_w, out_norm)


def _xattn_kernel(q_ref, kv_ref, o_ref):
    for h in range(X_HEADS):
        cols = slice(h * X_HEAD_DIM, (h + 1) * X_HEAD_DIM)
        k = kv_ref[:, cols]
        v = kv_ref[:, X_WIDTH + h * X_HEAD_DIM:X_WIDTH + (h + 1) * X_HEAD_DIM]
        s = lax.dot_general(q_ref[:, cols], k, (((1,), (1,)), ((), ())),
                            preferred_element_type=F32) * (X_HEAD_DIM ** -0.5)
        p = jnp.exp(s - jnp.max(s, axis=-1, keepdims=True))
        den = jnp.sum(p, axis=-1, keepdims=True)
        o = jnp.dot(p.astype(BF16), v, preferred_element_type=F32)
        o_ref[:, cols] = (o / den).astype(o_ref.dtype)


def _xattn(q, kv, *, batch, seq, rows=512):
    rows = min(rows, seq)
    nt = seq // rows
    return pl.pallas_call(
        _xattn_kernel,
        grid=(batch, nt),
        in_specs=[pl.BlockSpec((rows, X_WIDTH), lambda b, t: (b * nt + t, 0)),
                  pl.BlockSpec((MEM_LEN, 2 * X_WIDTH), lambda b, t: (b, 0))],
        out_specs=pl.BlockSpec((rows, X_WIDTH), lambda b, t: (b * nt + t, 0)),
        out_shape=jax.ShapeDtypeStruct((batch * seq, X_WIDTH), BF16),
        compiler_params=_params(("parallel", "parallel")),
        name="xattn",
    )(q, kv)


def _pad_lanes(v):
    return jnp.pad(v, ((0, 0), (0, LANES - v.shape[1])))[:, None, :]


def kernel(x, mem, norm_mix_pre, w_in, attn_sinks, dn_conv_w, dn_a_log, dn_dt_bias, dn_out_norm,
           w_out, norm_mix_post, norm_mem, norm_x_pre, w_xq, w_xkv, w_xo, norm_x_post,
           norm_ffn_pre, w_gate_up, w_down, norm_ffn_post):
    batch, seq, d = x.shape
    tokens = batch * seq
    h = x.reshape(tokens, d)
    memf = mem.reshape(batch * MEM_LEN, d)
    gain = lambda g: g[:, None, :]
    g_mix_pre, g_mix_post, g_mem = gain(norm_mix_pre), gain(norm_mix_post), gain(norm_mem)
    g_x_pre, g_x_post = gain(norm_x_pre), gain(norm_x_post)
    g_ffn_pre, g_ffn_post = gain(norm_ffn_pre), gain(norm_ffn_post)
    g_dn_out = gain(dn_out_norm)
    alog = _pad_lanes(dn_a_log)
    dtb = _pad_lanes(dn_dt_bias)

    a = _norm_cast(h, g_mix_pre, 0, rows=256)
    for l in range(DEPTH):
        u = _matmul(a, w_in, l, col_off_blocks=0, n_cols=IN_MAIN_COLS, tm=2048, tn=512,
                    out_dtype=F32, name="in_proj")
        gates = _matmul(a, w_in, l, col_off_blocks=OFF_DA // LANES, n_cols=LANES, tm=2048, tn=LANES,
                        out_dtype=F32, n_valid=N_GATE_COLS, name="in_proj_gates")
        attn = _swa(u, attn_sinks, l, batch=batch, seq=seq)
        gb, gt = _dn_gates(gates, alog, dtb, l)
        dn = _deltanet(u, gb, gt, dn_conv_w, g_dn_out, l, batch=batch, seq=seq)
        mix_in = jnp.concatenate([attn, dn], axis=-1)
        mix = _matmul(mix_in, w_out, l, col_off_blocks=0, n_cols=d, tm=2048, tn=512,
                      out_dtype=F32, name="out_proj")
        h, a = _resnorm(mix, h, g_mix_post, l, g_x_pre, l)
        mem_n = _norm_cast(memf, g_mem, l, rows=256)
        kv = _matmul(mem_n, w_xkv, l, col_off_blocks=0, n_cols=2 * X_WIDTH, tm=batch * MEM_LEN, tn=512,
                     out_dtype=BF16, name="xkv_proj")
        qx = _matmul(a, w_xq, l, col_off_blocks=0, n_cols=X_WIDTH, tm=2048, tn=512,
                     out_dtype=BF16, name="xq_proj")
        xo = _xattn(qx, kv, batch=batch, seq=seq)
        xo = _matmul(xo, w_xo, l, col_off_blocks=0, n_cols=d, tm=2048, tn=512,
                     out_dtype=F32, name="xo_proj")
        h, a = _resnorm(xo, h, g_x_post, l, g_ffn_pre, l)
        hid = _matmul_swiglu(a, w_gate_up, l, tm=2048, tn=256)
        ff = _matmul(hid, w_down, l, col_off_blocks=0, n_cols=d, tm=1024, tn=256,
                     out_dtype=F32, name="ffn_down")
        if l + 1 < DEPTH:
            h, a = _resnorm(ff, h, g_ffn_post, l, g_mix_pre, l + 1)
        else:
            h, _ = _resnorm(ff, h, g_ffn_post, l, None, None)
    return h.reshape(batch, seq, d)
```

```python
import functools

import jax
import jax.numpy as jnp
from jax import lax
from jax.experimental import pallas as pl
from jax.experimental.pallas import tpu as pltpu

F32 = jnp.float32
BF16 = jnp.bfloat16

D_MODEL = 4096
DEPTH = 4
MEM_LEN = 256
ATTN_WIDTH = 2048
DN_WIDTH = 2048
ATTN_HEAD_DIM = 64
ATTN_Q_HEADS = 32
ATTN_KV_HEADS = 4
ATTN_GROUP = 8
ATTN_KV_WIDTH = 256
WINDOW = 128
ATTN_BLOCK = 128
DN_HEAD_DIM = 128
DN_HEADS = 16
CONV_WIDTH = 4
DN_CHUNK = 64
X_HEADS = 4
X_HEAD_DIM = 128
X_WIDTH = 512
FFN_HIDDEN = 11008
RMS_EPS = 1e-6

OFF_AQ = 0
OFF_AK = OFF_AQ + ATTN_WIDTH
OFF_AV = OFF_AK + ATTN_KV_WIDTH
OFF_DQKV = OFF_AV + ATTN_KV_WIDTH
OFF_DZ = OFF_DQKV + 3 * DN_WIDTH
OFF_DA = OFF_DZ + DN_WIDTH
IN_MAIN_COLS = OFF_DA
N_GATE_COLS = 2 * DN_HEADS

LANES = 128
SUBLANES = 8
VMEM_LIMIT_BYTES = 56 * 1024 * 1024


def _params(semantics, vmem=VMEM_LIMIT_BYTES):
    return pltpu.CompilerParams(dimension_semantics=semantics, vmem_limit_bytes=vmem)


def _rms(x, gain):
    ms = jnp.mean(x * x, axis=-1, keepdims=True)
    return x * lax.rsqrt(ms + RMS_EPS) * gain


def _norm_kernel(x_ref, g_ref, o_ref):
    o_ref[...] = _rms(x_ref[...], g_ref[...]).astype(o_ref.dtype)


def _norm_cast(x, gains, layer, *, rows):
    m, d = x.shape
    rows = min(rows, m)
    return pl.pallas_call(
        _norm_kernel,
        grid=(m // rows,),
        in_specs=[pl.BlockSpec((rows, d), lambda i: (i, 0)),
                  pl.BlockSpec((None, 1, d), lambda i: (layer, 0, 0))],
        out_specs=pl.BlockSpec((rows, d), lambda i: (i, 0)),
        out_shape=jax.ShapeDtypeStruct((m, d), BF16),
        compiler_params=_params(("parallel",)),
        name="norm_cast",
    )(x, gains)


def _resnorm_kernel(y_ref, h_ref, gp_ref, gn_ref, hn_ref, a_ref):
    hn = h_ref[...] + _rms(y_ref[...].astype(F32), gp_ref[...])
    hn_ref[...] = hn
    a_ref[...] = _rms(hn, gn_ref[...]).astype(a_ref.dtype)


def _resnorm_last_kernel(y_ref, h_ref, gp_ref, hn_ref):
    hn_ref[...] = h_ref[...] + _rms(y_ref[...].astype(F32), gp_ref[...])


def _resnorm(y, h, g_post, l_post, g_next, l_next, *, rows=128):
    m, d = y.shape
    row_spec = pl.BlockSpec((rows, d), lambda i: (i, 0))
    if g_next is None:
        return pl.pallas_call(
            _resnorm_last_kernel,
            grid=(m // rows,),
            in_specs=[row_spec, row_spec, pl.BlockSpec((None, 1, d), lambda i: (l_post, 0, 0))],
            out_specs=row_spec,
            out_shape=jax.ShapeDtypeStruct((m, d), F32),
            compiler_params=_params(("parallel",)),
            name="resnorm_last",
        )(y, h, g_post), None
    return pl.pallas_call(
        _resnorm_kernel,
        grid=(m // rows,),
        in_specs=[row_spec, row_spec,
                  pl.BlockSpec((None, 1, d), lambda i: (l_post, 0, 0)),
                  pl.BlockSpec((None, 1, d), lambda i: (l_next, 0, 0))],
        out_specs=[row_spec, row_spec],
        out_shape=[jax.ShapeDtypeStruct((m, d), F32), jax.ShapeDtypeStruct((m, d), BF16)],
        compiler_params=_params(("parallel",)),
        name="resnorm",
    )(y, h, g_post, g_next)


def _row_resident_spec(tm, k):
    return pl.BlockSpec((tm, k), lambda i, j: (i, 0), pipeline_mode=pl.Buffered(1))


def _mm_kernel(x_ref, w_ref, o_ref, *, n_valid):
    acc = jnp.dot(x_ref[...], w_ref[...].astype(BF16), preferred_element_type=F32)
    if n_valid is not None:
        lane = lax.broadcasted_iota(jnp.int32, acc.shape, 1)
        acc = jnp.where(lane < n_valid, acc, 0.0)
    o_ref[...] = acc.astype(o_ref.dtype)


def _matmul(x, w, layer, *, col_off_blocks, n_cols, tm, tn, out_dtype, n_valid=None, name):
    m, k = x.shape
    tm = min(tm, m)
    assert w.shape[1] == k and m % tm == 0 and n_cols % tn == 0
    return pl.pallas_call(
        functools.partial(_mm_kernel, n_valid=n_valid),
        grid=(m // tm, n_cols // tn),
        in_specs=[_row_resident_spec(tm, k),
                  pl.BlockSpec((None, k, tn), lambda i, j: (layer, 0, j + col_off_blocks))],
        out_specs=pl.BlockSpec((tm, tn), lambda i, j: (i, j)),
        out_shape=jax.ShapeDtypeStruct((m, n_cols), out_dtype),
        compiler_params=_params(("parallel", "arbitrary")),
        name=name,
    )(x, w)


def _mm_nt_kernel(x_ref, wt_ref, o_ref, *, n_valid):
    acc = lax.dot_general(x_ref[...], wt_ref[...].astype(BF16), (((1,), (1,)), ((), ())),
                          preferred_element_type=F32)
    if n_valid is not None:
        lane = lax.broadcasted_iota(jnp.int32, acc.shape, 1)
        acc = jnp.where(lane < n_valid, acc, 0.0)
    o_ref[...] = acc.astype(o_ref.dtype)


def _matmul_nt(x, wt, layer, *, row_off_blocks, n_cols, tm, tn, out_dtype, n_valid=None, name):
    m, k = x.shape
    tm = min(tm, m)
    assert wt.shape[2] == k and m % tm == 0 and n_cols % tn == 0
    return pl.pallas_call(
        functools.partial(_mm_nt_kernel, n_valid=n_valid),
        grid=(m // tm, n_cols // tn),
        in_specs=[_row_resident_spec(tm, k),
                  pl.BlockSpec((None, tn, k), lambda i, j: (layer, j + row_off_blocks, 0))],
        out_specs=pl.BlockSpec((tm, tn), lambda i, j: (i, j)),
        out_shape=jax.ShapeDtypeStruct((m, n_cols), out_dtype),
        compiler_params=_params(("parallel", "arbitrary")),
        name=name,
    )(x, wt)


def _mm2_kernel(x1_ref, x2_ref, w_ref, o_ref):
    k1 = x1_ref.shape[1]
    w = w_ref[...].astype(BF16)
    acc = jnp.dot(x1_ref[...], w[:k1], preferred_element_type=F32)
    acc = acc + jnp.dot(x2_ref[...], w[k1:], preferred_element_type=F32)
    o_ref[...] = acc.astype(o_ref.dtype)


def _matmul_cat(x1, x2, w, layer, *, tm, tn, out_dtype, name):
    m, k1 = x1.shape
    k2 = x2.shape[1]
    n = w.shape[2]
    tm = min(tm, m)
    assert w.shape[1] == k1 + k2 and m % tm == 0 and n % tn == 0
    return pl.pallas_call(
        _mm2_kernel,
        grid=(m // tm, n // tn),
        in_specs=[_row_resident_spec(tm, k1), _row_resident_spec(tm, k2),
                  pl.BlockSpec((None, k1 + k2, tn), lambda i, j: (layer, 0, j))],
        out_specs=pl.BlockSpec((tm, tn), lambda i, j: (i, j)),
        out_shape=jax.ShapeDtypeStruct((m, n), out_dtype),
        compiler_params=_params(("parallel", "arbitrary")),
        name=name,
    )(x1, x2, w)


def _swiglu_kernel(x_ref, wg_ref, wu_ref, o_ref):
    x = x_ref[...]
    g = jnp.dot(x, wg_ref[...].astype(BF16), preferred_element_type=F32)
    u = jnp.dot(x, wu_ref[...].astype(BF16), preferred_element_type=F32)
    o_ref[...] = (g * jax.nn.sigmoid(g) * u).astype(o_ref.dtype)


def _matmul_swiglu(x, w, layer, *, tm, tn):
    m, k = x.shape
    tm = min(tm, m)
    hid = w.shape[2] // 2
    assert hid % tn == 0 and m % tm == 0
    up_off = hid // tn
    return pl.pallas_call(
        _swiglu_kernel,
        grid=(m // tm, hid // tn),
        in_specs=[_row_resident_spec(tm, k),
                  pl.BlockSpec((None, k, tn), lambda i, j: (layer, 0, j)),
                  pl.BlockSpec((None, k, tn), lambda i, j: (layer, 0, j + up_off))],
        out_specs=pl.BlockSpec((tm, tn), lambda i, j: (i, j)),
        out_shape=jax.ShapeDtypeStruct((m, hid), BF16),
        compiler_params=_params(("parallel", "arbitrary")),
        name="ffn_gate_up",
    )(x, w, w)


def _alibi_slope(head):
    return 2.0 ** (-8.0 * (head + 1) / ATTN_Q_HEADS)


def _block_diag_pair(blk, use_right):
    lane = lax.broadcasted_iota(jnp.int32, blk.shape, 1)
    left = lane < ATTN_HEAD_DIM
    rolled = pltpu.roll(blk, ATTN_HEAD_DIM, 1)
    if use_right:
        top, bot = jnp.where(left, rolled, 0.0), jnp.where(left, 0.0, blk)
    else:
        top, bot = jnp.where(left, blk, 0.0), jnp.where(left, 0.0, rolled)
    return jnp.concatenate([top, bot], axis=0)


def _swa_kernel(sink_ref, q_ref, kp_ref, kc_ref, vp_ref, vc_ref, o_ref, *, layer):
    n = pl.program_id(1)
    blk = ATTN_BLOCK
    kw = jnp.concatenate([kp_ref[...], kc_ref[...]], axis=0)
    vw = jnp.concatenate([vp_ref[...], vc_ref[...]], axis=0)
    r = lax.broadcasted_iota(jnp.int32, (blk, 2 * blk), 0)
    c = lax.broadcasted_iota(jnp.int32, (blk, 2 * blk), 1)
    dist = r + blk - c
    valid = (dist >= 0) & (dist < WINDOW) & ((n > 0) | (c >= blk))
    distf = dist.astype(F32)
    out_lane = lax.broadcasted_iota(jnp.int32, (blk, LANES), 1)
    for kv_pair in range(ATTN_KV_HEADS // 2):
        kblk = kw[:, kv_pair * LANES:(kv_pair + 1) * LANES]
        vblk = vw[:, kv_pair * LANES:(kv_pair + 1) * LANES]
        for e in range(2):
            kv_head = 2 * kv_pair + e
            kext = _block_diag_pair(kblk, e == 1).astype(BF16)
            vext = _block_diag_pair(vblk, e == 1).astype(BF16)
            for qp in range(ATTN_GROUP // 2):
                head_a = kv_head * ATTN_GROUP + 2 * qp
                cols = slice(head_a * ATTN_HEAD_DIM, (head_a + 2) * ATTN_HEAD_DIM)
                q2 = (q_ref[:, cols] * (ATTN_HEAD_DIM ** -0.5)).astype(BF16)
                s2 = lax.dot_general(q2, kext, (((1,), (1,)), ((), ())),
                                     preferred_element_type=F32)
                probs, inv_den = [], []
                for t in range(2):
                    head = head_a + t
                    s = s2[:, t * 2 * blk:(t + 1) * 2 * blk] - _alibi_slope(head) * distf
                    s = jnp.where(valid, s, -jnp.inf)
                    sink = sink_ref[layer, head]
                    mx = jnp.maximum(jnp.max(s, axis=-1, keepdims=True), sink)
                    p = jnp.exp(s - mx)
                    den = jnp.sum(p, axis=-1, keepdims=True) + jnp.exp(sink - mx)
                    probs.append(p)
                    inv_den.append(1.0 / den)
                p2 = jnp.concatenate(probs, axis=1).astype(BF16)
                o2 = jnp.dot(p2, vext, preferred_element_type=F32)
                scale = jnp.where(out_lane < ATTN_HEAD_DIM, inv_den[0], inv_den[1])
                o_ref[:, cols] = (o2 * scale).astype(o_ref.dtype)


def _swa(u, sinks, layer, *, batch, seq):
    nb = seq // ATTN_BLOCK
    kcol = OFF_AK // ATTN_KV_WIDTH
    vcol = OFF_AV // ATTN_KV_WIDTH

    def cur(col):
        return lambda b, n: (b * nb + n, col)

    def prev(col):
        return lambda b, n: (b * nb + jnp.maximum(n - 1, 0), col)

    kv_shape = (ATTN_BLOCK, ATTN_KV_WIDTH)
    return pl.pallas_call(
        functools.partial(_swa_kernel, layer=layer),
        grid=(batch, nb),
        in_specs=[pl.BlockSpec(memory_space=pltpu.SMEM),
                  pl.BlockSpec((ATTN_BLOCK, ATTN_WIDTH), lambda b, n: (b * nb + n, 0)),
                  pl.BlockSpec(kv_shape, prev(kcol)), pl.BlockSpec(kv_shape, cur(kcol)),
                  pl.BlockSpec(kv_shape, prev(vcol)), pl.BlockSpec(kv_shape, cur(vcol))],
        out_specs=pl.BlockSpec((ATTN_BLOCK, ATTN_WIDTH), lambda b, n: (b * nb + n, 0)),
        out_shape=jax.ShapeDtypeStruct((batch * seq, ATTN_WIDTH), BF16),
        compiler_params=_params(("parallel", "parallel")),
        name="swa",
    )(sinks, u, u, u, u, u)


def _dn_gates_kernel(g_ref, alog_ref, bias_ref, gb_ref, gt_ref):
    raw = g_ref[...]
    rows = raw.shape[0]
    lane = lax.broadcasted_iota(jnp.int32, raw.shape, 1)
    g = -jnp.exp(alog_ref[...]) * jax.nn.softplus(raw + bias_ref[...])
    pos = lax.broadcasted_iota(jnp.int32, raw.shape, 0) % DN_CHUNK
    acc = g
    shift = 1
    while shift < DN_CHUNK:
        acc = acc + jnp.where(pos >= shift, pltpu.roll(acc, shift, 0), 0.0)
        shift *= 2
    gb = jnp.where(lane < DN_HEADS, acc, jnp.where(lane < N_GATE_COLS, jax.nn.sigmoid(raw), 0.0))
    gb_ref[...] = gb
    gt_ref[...] = jnp.swapaxes(gb.reshape(rows // DN_CHUNK, DN_CHUNK, LANES), 1, 2)


def _dn_gates(gates, alog, bias, layer, *, rows=512):
    m = gates.shape[0]
    rows = min(rows, m)
    nc = rows // DN_CHUNK
    return pl.pallas_call(
        _dn_gates_kernel,
        grid=(m // rows,),
        in_specs=[pl.BlockSpec((rows, LANES), lambda i: (i, 0)),
                  pl.BlockSpec((None, 1, LANES), lambda i: (layer, 0, 0)),
                  pl.BlockSpec((None, 1, LANES), lambda i: (layer, 0, 0))],
        out_specs=[pl.BlockSpec((rows, LANES), lambda i: (i, 0)),
                   pl.BlockSpec((nc, LANES, DN_CHUNK), lambda i: (i, 0, 0))],
        out_shape=[jax.ShapeDtypeStruct((m, LANES), F32),
                   jax.ShapeDtypeStruct((m // DN_CHUNK, LANES, DN_CHUNK), F32)],
        compiler_params=_params(("parallel",)),
        name="dn_gates",
    )(gates, alog, bias)


def _bdot(eq, a, b):
    return jnp.einsum(eq, a.astype(BF16), b.astype(BF16), preferred_element_type=F32)


def _unit_lower_inverse(low, ri, ci):
    base = SUBLANES
    same = lambda b: (ri & -b) == (ci & -b)
    x = -jnp.where(same(base), low, 0.0)
    inv = (ri == ci).astype(F32) + x
    size = 2
    while size < base:
        x = _bdot('cij,cjk->cik', x, x)
        inv = inv + _bdot('cij,cjk->cik', inv, x)
        size *= 2
    blk = base
    while blk < DN_CHUNK:
        off = jnp.where(same(2 * blk) & ~same(blk), low, 0.0)
        inv = inv - _bdot('cij,cjk->cik', inv, _bdot('cij,cjk->cik', off, inv))
        blk *= 2
    return inv


def _dn_kernel(xq_ref, xk_ref, xv_ref, z_ref, gb_ref, gt_ref, cwq_ref, cwk_ref, cwv_ref, onorm_ref,
               o_ref, state_ref, tail_ref, *, heads_per_step, rows):
    hg = pl.program_id(1)
    t = pl.program_id(2)
    nc = rows // DN_CHUNK
    dh = DN_HEAD_DIM

    @pl.when(t == 0)
    def _():
        state_ref[...] = jnp.zeros_like(state_ref)
        tail_ref[...] = jnp.zeros_like(tail_ref)

    def conv_silu(x_ref, cw_ref, idx):
        x = x_ref[...]
        ext = jnp.concatenate([tail_ref[idx], x], axis=0)
        w = cw_ref[...]
        y = x * w[CONV_WIDTH - 1:CONV_WIDTH]
        for s in range(1, CONV_WIDTH):
            y = y + pltpu.roll(ext, s, 0)[SUBLANES:] * w[CONV_WIDTH - 1 - s:CONV_WIDTH - s]
        tail_ref[idx] = x[rows - SUBLANES:]
        return y * jax.nn.sigmoid(y)

    yq = conv_silu(xq_ref, cwq_ref, 0)
    yk = conv_silu(xk_ref, cwk_ref, 1)
    yv = conv_silu(xv_ref, cwv_ref, 2)

    gball = gb_ref[...]
    lane = lax.broadcasted_iota(jnp.int32, gball.shape, 1)
    ri = lax.broadcasted_iota(jnp.int32, (DN_CHUNK, DN_CHUNK), 0)
    ci = lax.broadcasted_iota(jnp.int32, (DN_CHUNK, DN_CHUNK), 1)
    causal = ri >= ci
    strict = ri > ci

    neg_p_q, n_mat, o_loc, g_last = [], [], [], []
    for hh in range(heads_per_step):
        head = hg * heads_per_step + hh
        cols = slice(hh * dh, (hh + 1) * dh)
        q = yq[:, cols]
        k = yk[:, cols]
        v = yv[:, cols]
        q = q * lax.rsqrt(jnp.sum(q * q, axis=-1, keepdims=True) + 1e-6) * (dh ** -0.5)
        k = k * lax.rsqrt(jnp.sum(k * k, axis=-1, keepdims=True) + 1e-6)
        gcum = jnp.sum(jnp.where(lane == head, gball, 0.0), axis=-1, keepdims=True)
        beta = jnp.sum(jnp.where(lane == head + DN_HEADS, gball, 0.0), axis=-1, keepdims=True)
        grow = gt_ref[:, pl.ds(head, 1), :]
        q3 = q.reshape(nc, DN_CHUNK, dh)
        k3 = k.reshape(nc, DN_CHUNK, dh)
        v3 = v.reshape(nc, DN_CHUNK, dh)
        gc3 = gcum.reshape(nc, DN_CHUNK, 1)
        b3 = beta.reshape(nc, DN_CHUNK, 1)
        decay = jnp.exp(jnp.where(causal, gc3 - grow, -jnp.inf))
        eg = jnp.exp(gc3)
        gl = gc3[:, DN_CHUNK - 1:DN_CHUNK, :]
        kb = k3 * b3
        kk = _bdot('cid,cjd->cij', kb, k3)
        qk = _bdot('cid,cjd->cij', q3, k3)
        low = jnp.where(strict, kk * decay, 0.0)
        a_intra = qk * decay
        tinv = _unit_lower_inverse(low, ri, ci)
        rhs = jnp.concatenate([kb * eg, v3 * b3], axis=-1)
        wu = _bdot('cij,cjd->cid', tinv, rhs)
        kd = k3 * jnp.exp(gl - gc3)
        pn = _bdot('ncd,nce->nde', kd, wu)
        aq = _bdot('cij,cjd->cid', a_intra, wu)
        qeff = q3 * eg - aq[..., :dh]
        neg_p_q.append(jnp.concatenate([-pn[..., :dh], qeff], axis=1))
        n_mat.append(pn[..., dh:])
        o_loc.append(aq[..., dh:])
        g_last.append(jnp.exp(gl))

    states = [state_ref[hh] for hh in range(heads_per_step)]
    outs = [[] for _ in range(heads_per_step)]
    for cidx in range(nc):
        for hh in range(heads_per_step):
            s = states[hh]
            r = jnp.dot(neg_p_q[hh][cidx].astype(BF16), s.astype(BF16), preferred_element_type=F32)
            outs[hh].append(r[dh:] + o_loc[hh][cidx])
            states[hh] = g_last[hh][cidx] * s + r[:dh] + n_mat[hh][cidx]
    for hh in range(heads_per_step):
        state_ref[hh] = states[hh]
        cols = slice(hh * dh, (hh + 1) * dh)
        o = jnp.concatenate(outs[hh], axis=0)
        z = z_ref[:, cols]
        o_ref[:, cols] = (_rms(o, onorm_ref[...]) * (z * jax.nn.sigmoid(z))).astype(o_ref.dtype)


def _deltanet(u, gb, gt, conv_w, out_norm, layer, *, batch, seq, heads_per_step=4, rows=512):
    wcols = heads_per_step * DN_HEAD_DIM
    rows = min(rows, seq)
    nt = seq // rows
    nc = rows // DN_CHUNK
    n_hg = DN_HEADS // heads_per_step
    qoff = OFF_DQKV // wcols
    koff = (OFF_DQKV + DN_WIDTH) // wcols
    voff = (OFF_DQKV + 2 * DN_WIDTH) // wcols
    zoff = OFF_DZ // wcols

    def xspec(off):
        return pl.BlockSpec((rows, wcols), lambda b, g, t: (b * nt + t, off + g))

    def cwspec(off):
        return pl.BlockSpec((None, CONV_WIDTH, wcols), lambda b, g, t: (layer, 0, off + g))

    return pl.pallas_call(
        functools.partial(_dn_kernel, heads_per_step=heads_per_step, rows=rows),
        grid=(batch, n_hg, nt),
        in_specs=[xspec(qoff), xspec(koff), xspec(voff), xspec(zoff),
                  pl.BlockSpec((rows, LANES), lambda b, g, t: (b * nt + t, 0)),
                  pl.BlockSpec((nc, LANES, DN_CHUNK), lambda b, g, t: (b * nt + t, 0, 0)),
                  cwspec(0), cwspec(DN_WIDTH // wcols), cwspec(2 * DN_WIDTH // wcols),
                  pl.BlockSpec((None, 1, DN_HEAD_DIM), lambda b, g, t: (layer, 0, 0))],
        out_specs=pl.BlockSpec((rows, wcols), lambda b, g, t: (b * nt + t, g)),
        out_shape=jax.ShapeDtypeStruct((batch * seq, DN_WIDTH), BF16),
        scratch_shapes=[pltpu.VMEM((heads_per_step, DN_HEAD_DIM, DN_HEAD_DIM), F32),
                        pltpu.VMEM((3, SUBLANES, wcols), F32)],
        compiler_params=_params(("parallel", "parallel", "arbitrary")),
        name="deltanet",
    )(u, u, u, u, gb, gt, conv_w, conv_w, conv_w, out_norm)


def _xattn_out_kernel(q_ref, kv_ref, wo_ref, h_ref, gp_ref, gn_ref, hn_ref, a_ref, wo_bf16_ref):
    @pl.when((pl.program_id(0) == 0) & (pl.program_id(1) == 0))
    def _():
        wo_bf16_ref[...] = wo_ref[...].astype(BF16)

    heads = []
    for h in range(X_HEADS):
        cols = slice(h * X_HEAD_DIM, (h + 1) * X_HEAD_DIM)
        k = kv_ref[:, cols]
        v = kv_ref[:, X_WIDTH + h * X_HEAD_DIM:X_WIDTH + (h + 1) * X_HEAD_DIM]
        s = lax.dot_general(q_ref[:, cols], k, (((1,), (1,)), ((), ())),
                            preferred_element_type=F32) * (X_HEAD_DIM ** -0.5)
        p = jnp.exp(s - jnp.max(s, axis=-1, keepdims=True))
        den = jnp.sum(p, axis=-1, keepdims=True)
        o = jnp.dot(p.astype(BF16), v, preferred_element_type=F32)
        heads.append((o / den).astype(BF16))
    y = jnp.dot(jnp.concatenate(heads, axis=1), wo_bf16_ref[...], preferred_element_type=F32)
    hn = h_ref[...] + _rms(y, gp_ref[...])
    hn_ref[...] = hn
    a_ref[...] = _rms(hn, gn_ref[...]).astype(a_ref.dtype)


def _xattn_out(q, kv, w_o, h, g_post, g_next, layer, *, batch, seq, rows=256):
    rows = min(rows, seq)
    nt = seq // rows
    d = h.shape[1]
    row_spec = pl.BlockSpec((rows, d), lambda b, t: (b * nt + t, 0))
    gain_spec = pl.BlockSpec((None, 1, d), lambda b, t: (layer, 0, 0))
    return pl.pallas_call(
        _xattn_out_kernel,
        grid=(batch, nt),
        in_specs=[pl.BlockSpec((rows, X_WIDTH), lambda b, t: (b * nt + t, 0)),
                  pl.BlockSpec((MEM_LEN, 2 * X_WIDTH), lambda b, t: (b, 0)),
                  pl.BlockSpec((None, X_WIDTH, d), lambda b, t: (layer, 0, 0)),
                  row_spec, gain_spec, gain_spec],
        out_specs=[row_spec, row_spec],
        out_shape=[jax.ShapeDtypeStruct((batch * seq, d), F32), jax.ShapeDtypeStruct((batch * seq, d), BF16)],
        scratch_shapes=[pltpu.VMEM((X_WIDTH, d), BF16)],
        compiler_params=_params(("arbitrary", "arbitrary")),
        name="xattn_out",
    )(q, kv, w_o, h, g_post, g_next)


def _pad_lanes(v):
    return jnp.pad(v, ((0, 0), (0, LANES - v.shape[1])))[:, None, :]


def kernel(x, mem, norm_mix_pre, w_in, attn_sinks, dn_conv_w, dn_a_log, dn_dt_bias, dn_out_norm,
           w_out, norm_mix_post, norm_mem, norm_x_pre, w_xq, w_xkv, w_xo, norm_x_post,
           norm_ffn_pre, w_gate_up, w_down, norm_ffn_post):
    batch, seq, d = x.shape
    tokens = batch * seq
    h = x.reshape(tokens, d)
    memf = mem.reshape(batch * MEM_LEN, d)
    gain = lambda g: g[:, None, :]
    g_mix_pre, g_mix_post, g_mem = gain(norm_mix_pre), gain(norm_mix_post), gain(norm_mem)
    g_x_pre, g_x_post = gain(norm_x_pre), gain(norm_x_post)
    g_ffn_pre, g_ffn_post = gain(norm_ffn_pre), gain(norm_ffn_post)
    g_dn_out = gain(dn_out_norm)
    alog = _pad_lanes(dn_a_log)
    dtb = _pad_lanes(dn_dt_bias)
    w_in_t = jnp.swapaxes(w_in, 1, 2)

    a = _norm_cast(h, g_mix_pre, 0, rows=256)
    for l in range(DEPTH):
        u = _matmul_nt(a, w_in_t, l, row_off_blocks=0, n_cols=IN_MAIN_COLS, tm=2048, tn=512,
                       out_dtype=F32, name="in_proj")
        gates = _matmul_nt(a, w_in_t, l, row_off_blocks=OFF_DA // LANES, n_cols=LANES, tm=2048, tn=LANES,
                           out_dtype=F32, n_valid=N_GATE_COLS, name="in_proj_gates")
        attn = _swa(u, attn_sinks, l, batch=batch, seq=seq)
        gb, gt = _dn_gates(gates, alog, dtb, l)
        dn = _deltanet(u, gb, gt, dn_conv_w, g_dn_out, l, batch=batch, seq=seq)
        mix = _matmul_cat(attn, dn, w_out, l, tm=2048, tn=512, out_dtype=BF16, name="out_proj")
        h, a = _resnorm(mix, h, g_mix_post, l, g_x_pre, l)
        mem_n = _norm_cast(memf, g_mem, l, rows=256)
        kv = _matmul(mem_n, w_xkv, l, col_off_blocks=0, n_cols=2 * X_WIDTH, tm=batch * MEM_LEN, tn=512,
                     out_dtype=BF16, name="xkv_proj")
        qx = _matmul(a, w_xq, l, col_off_blocks=0, n_cols=X_WIDTH, tm=2048, tn=512,
                     out_dtype=BF16, name="xq_proj")
        h, a = _xattn_out(qx, kv, w_xo, h, g_x_post, g_ffn_pre, l, batch=batch, seq=seq)
        hid = _matmul_swiglu(a, w_gate_up, l, tm=2048, tn=256)
        ff = _matmul(hid, w_down, l, col_off_blocks=0, n_cols=d, tm=1024, tn=256,
                     out_dtype=BF16, name="ffn_down")
        if l + 1 < DEPTH:
            h, a = _resnorm(ff, h, g_ffn_post, l, g_mix_pre, l + 1)
        else:
            h, _ = _resnorm(ff, h, g_ffn_post, l, None, None)
    return h.reshape(batch, seq, d)
```

```python
import functools

import jax
import jax.numpy as jnp
from jax import lax
from jax.experimental import pallas as pl
from jax.experimental.pallas import tpu as pltpu

F32 = jnp.float32
BF16 = jnp.bfloat16

D_MODEL = 4096
DEPTH = 4
MEM_LEN = 256
ATTN_WIDTH = 2048
DN_WIDTH = 2048
ATTN_HEAD_DIM = 64
ATTN_Q_HEADS = 32
ATTN_KV_HEADS = 4
ATTN_GROUP = 8
ATTN_KV_WIDTH = 256
WINDOW = 128
ATTN_BLOCK = 128
DN_HEAD_DIM = 128
DN_HEADS = 16
CONV_WIDTH = 4
DN_CHUNK = 64
X_HEADS = 4
X_HEAD_DIM = 128
X_WIDTH = 512
FFN_HIDDEN = 11008
RMS_EPS = 1e-6

OFF_AQ = 0
OFF_AK = OFF_AQ + ATTN_WIDTH
OFF_AV = OFF_AK + ATTN_KV_WIDTH
OFF_DQKV = OFF_AV + ATTN_KV_WIDTH
OFF_DZ = OFF_DQKV + 3 * DN_WIDTH
OFF_DA = OFF_DZ + DN_WIDTH
IN_MAIN_COLS = OFF_DA
N_GATE_COLS = 2 * DN_HEADS

LANES = 128
SUBLANES = 8
VMEM_LIMIT_BYTES = 56 * 1024 * 1024


def _params(semantics, vmem=VMEM_LIMIT_BYTES):
    return pltpu.CompilerParams(dimension_semantics=semantics, vmem_limit_bytes=vmem)


def _rms(x, gain):
    ms = jnp.mean(x * x, axis=-1, keepdims=True)
    return x * lax.rsqrt(ms + RMS_EPS) * gain


def _norm_kernel(x_ref, g_ref, o_ref):
    o_ref[...] = _rms(x_ref[...], g_ref[...]).astype(o_ref.dtype)


def _norm_cast(x, gains, layer, *, rows):
    m, d = x.shape
    rows = min(rows, m)
    return pl.pallas_call(
        _norm_kernel,
        grid=(m // rows,),
        in_specs=[pl.BlockSpec((rows, d), lambda i: (i, 0)),
                  pl.BlockSpec((None, 1, d), lambda i: (layer, 0, 0))],
        out_specs=pl.BlockSpec((rows, d), lambda i: (i, 0)),
        out_shape=jax.ShapeDtypeStruct((m, d), BF16),
        compiler_params=_params(("parallel",)),
        name="norm_cast",
    )(x, gains)


def _resnorm_kernel(y_ref, h_ref, gp_ref, gn_ref, hn_ref, a_ref):
    hn = h_ref[...] + _rms(y_ref[...].astype(F32), gp_ref[...])
    hn_ref[...] = hn
    a_ref[...] = _rms(hn, gn_ref[...]).astype(a_ref.dtype)


def _resnorm_proj_kernel(y_ref, h_ref, gp_ref, gn_ref, w_ref, hn_ref, q_ref, w_bf16_ref):
    @pl.when(pl.program_id(0) == 0)
    def _():
        w_bf16_ref[...] = w_ref[...].astype(BF16)

    hn = h_ref[...] + _rms(y_ref[...].astype(F32), gp_ref[...])
    hn_ref[...] = hn
    a = _rms(hn, gn_ref[...]).astype(BF16)
    q_ref[...] = jnp.dot(a, w_bf16_ref[...], preferred_element_type=F32).astype(q_ref.dtype)


def _resnorm_proj(y, h, g_post, g_next, w, layer, *, rows=256):
    m, d = y.shape
    n = w.shape[2]
    rows = min(rows, m)
    row_spec = pl.BlockSpec((rows, d), lambda i: (i, 0))
    gain_spec = pl.BlockSpec((None, 1, d), lambda i: (layer, 0, 0))
    return pl.pallas_call(
        _resnorm_proj_kernel,
        grid=(m // rows,),
        in_specs=[row_spec, row_spec, gain_spec, gain_spec,
                  pl.BlockSpec((None, d, n), lambda i: (layer, 0, 0))],
        out_specs=[row_spec, pl.BlockSpec((rows, n), lambda i: (i, 0))],
        out_shape=[jax.ShapeDtypeStruct((m, d), F32), jax.ShapeDtypeStruct((m, n), BF16)],
        scratch_shapes=[pltpu.VMEM((d, n), BF16)],
        compiler_params=_params(("arbitrary",)),
        name="resnorm_xq",
    )(y, h, g_post, g_next, w)


def _resnorm_last_kernel(y_ref, h_ref, gp_ref, hn_ref):
    hn_ref[...] = h_ref[...] + _rms(y_ref[...].astype(F32), gp_ref[...])


def _resnorm(y, h, g_post, l_post, g_next, l_next, *, rows=128):
    m, d = y.shape
    row_spec = pl.BlockSpec((rows, d), lambda i: (i, 0))
    if g_next is None:
        return pl.pallas_call(
            _resnorm_last_kernel,
            grid=(m // rows,),
            in_specs=[row_spec, row_spec, pl.BlockSpec((None, 1, d), lambda i: (l_post, 0, 0))],
            out_specs=row_spec,
            out_shape=jax.ShapeDtypeStruct((m, d), F32),
            compiler_params=_params(("parallel",)),
            name="resnorm_last",
        )(y, h, g_post), None
    return pl.pallas_call(
        _resnorm_kernel,
        grid=(m // rows,),
        in_specs=[row_spec, row_spec,
                  pl.BlockSpec((None, 1, d), lambda i: (l_post, 0, 0)),
                  pl.BlockSpec((None, 1, d), lambda i: (l_next, 0, 0))],
        out_specs=[row_spec, row_spec],
        out_shape=[jax.ShapeDtypeStruct((m, d), F32), jax.ShapeDtypeStruct((m, d), BF16)],
        compiler_params=_params(("parallel",)),
        name="resnorm",
    )(y, h, g_post, g_next)


ROW_TILE_PREFETCH_BYTES = 16 * 1024 * 1024


def _row_resident_spec(tm, k):
    if 2 * tm * k * 2 <= ROW_TILE_PREFETCH_BYTES:
        return pl.BlockSpec((tm, k), lambda i, j: (i, 0))
    return pl.BlockSpec((tm, k), lambda i, j: (i, 0), pipeline_mode=pl.Buffered(1))


def _mm_kernel(x_ref, w_ref, o_ref):
    acc = jnp.dot(x_ref[...], w_ref[...].astype(BF16), preferred_element_type=F32)
    o_ref[...] = acc.astype(o_ref.dtype)


def _matmul(x, w, layer, *, col_off_blocks, n_cols, tm, tn, out_dtype, name):
    m, k = x.shape
    tm = min(tm, m)
    assert w.shape[1] == k and m % tm == 0 and n_cols % tn == 0
    assert (col_off_blocks * tn + n_cols) <= w.shape[2]
    return pl.pallas_call(
        _mm_kernel,
        grid=(m // tm, n_cols // tn),
        in_specs=[_row_resident_spec(tm, k),
                  pl.BlockSpec((None, k, tn), lambda i, j: (layer, 0, j + col_off_blocks))],
        out_specs=pl.BlockSpec((tm, tn), lambda i, j: (i, j)),
        out_shape=jax.ShapeDtypeStruct((m, n_cols), out_dtype),
        compiler_params=_params(("parallel", "arbitrary")),
        name=name,
    )(x, w)


def _mm_nt_kernel(x_ref, wt_ref, o_ref):
    acc = lax.dot_general(x_ref[...], wt_ref[...].astype(BF16), (((1,), (1,)), ((), ())),
                          preferred_element_type=F32)
    o_ref[...] = acc.astype(o_ref.dtype)


def _matmul_nt(x, wt, layer, *, row_off_blocks, n_cols, tm, tn, out_dtype, name):
    m, k = x.shape
    tm = min(tm, m)
    assert wt.shape[2] == k and m % tm == 0 and n_cols % tn == 0
    assert (row_off_blocks * tn + n_cols) <= wt.shape[1]
    return pl.pallas_call(
        _mm_nt_kernel,
        grid=(m // tm, n_cols // tn),
        in_specs=[_row_resident_spec(tm, k),
                  pl.BlockSpec((None, tn, k), lambda i, j: (layer, j + row_off_blocks, 0))],
        out_specs=pl.BlockSpec((tm, tn), lambda i, j: (i, j)),
        out_shape=jax.ShapeDtypeStruct((m, n_cols), out_dtype),
        compiler_params=_params(("parallel", "arbitrary")),
        name=name,
    )(x, wt)


def _mm2_kernel(x1_ref, x2_ref, w_ref, o_ref):
    k1 = x1_ref.shape[1]
    w = w_ref[...].astype(BF16)
    acc = jnp.dot(x1_ref[...], w[:k1], preferred_element_type=F32)
    acc = acc + jnp.dot(x2_ref[...], w[k1:], preferred_element_type=F32)
    o_ref[...] = acc.astype(o_ref.dtype)


def _matmul_cat(x1, x2, w, layer, *, tm, tn, out_dtype, name):
    m, k1 = x1.shape
    k2 = x2.shape[1]
    n = w.shape[2]
    tm = min(tm, m)
    assert w.shape[1] == k1 + k2 and m % tm == 0 and n % tn == 0
    return pl.pallas_call(
        _mm2_kernel,
        grid=(m // tm, n // tn),
        in_specs=[_row_resident_spec(tm, k1), _row_resident_spec(tm, k2),
                  pl.BlockSpec((None, k1 + k2, tn), lambda i, j: (layer, 0, j))],
        out_specs=pl.BlockSpec((tm, tn), lambda i, j: (i, j)),
        out_shape=jax.ShapeDtypeStruct((m, n), out_dtype),
        compiler_params=_params(("parallel", "arbitrary")),
        name=name,
    )(x1, x2, w)


def _swiglu_kernel(x_ref, wg_ref, wu_ref, o_ref):
    x = x_ref[...]
    g = jnp.dot(x, wg_ref[...].astype(BF16), preferred_element_type=F32)
    u = jnp.dot(x, wu_ref[...].astype(BF16), preferred_element_type=F32)
    o_ref[...] = (g * jax.nn.sigmoid(g) * u).astype(o_ref.dtype)


def _matmul_swiglu(x, w, layer, *, tm, tn):
    m, k = x.shape
    tm = min(tm, m)
    hid = w.shape[2] // 2
    assert hid % tn == 0 and m % tm == 0
    up_off = hid // tn
    return pl.pallas_call(
        _swiglu_kernel,
        grid=(m // tm, hid // tn),
        in_specs=[_row_resident_spec(tm, k),
                  pl.BlockSpec((None, k, tn), lambda i, j: (layer, 0, j)),
                  pl.BlockSpec((None, k, tn), lambda i, j: (layer, 0, j + up_off))],
        out_specs=pl.BlockSpec((tm, tn), lambda i, j: (i, j)),
        out_shape=jax.ShapeDtypeStruct((m, hid), BF16),
        compiler_params=_params(("parallel", "arbitrary")),
        name="ffn_gate_up",
    )(x, w, w)


SWA_LOOKAHEAD = 2


def _alibi_slope(head):
    return 2.0 ** (-8.0 * (head + 1) / ATTN_Q_HEADS)


def _block_diag_pair(blk, use_right):
    lane = lax.broadcasted_iota(jnp.int32, blk.shape, 1)
    left = lane < ATTN_HEAD_DIM
    rolled = pltpu.roll(blk, ATTN_HEAD_DIM, 1)
    if use_right:
        top, bot = jnp.where(left, rolled, 0.0), jnp.where(left, 0.0, blk)
    else:
        top, bot = jnp.where(left, blk, 0.0), jnp.where(left, 0.0, rolled)
    return jnp.concatenate([top, bot], axis=0)


def _swa_kernel(sink_ref, q_ref, kp_ref, kc_ref, vp_ref, vc_ref, o_ref, *, layer):
    n = pl.program_id(1)
    blk = ATTN_BLOCK
    kw = jnp.concatenate([kp_ref[...], kc_ref[...]], axis=0)
    vw = jnp.concatenate([vp_ref[...], vc_ref[...]], axis=0)
    r = lax.broadcasted_iota(jnp.int32, (blk, blk), 0)
    c = lax.broadcasted_iota(jnp.int32, (blk, blk), 1)
    upper = c > r
    dist = jnp.where(upper, r + blk - c, r - c).astype(F32)
    dist = jnp.where(upper & (n == 0), jnp.inf, dist)
    out_lane = lax.broadcasted_iota(jnp.int32, (blk, LANES), 1)

    kv_operands = []
    for kv_head in range(ATTN_KV_HEADS):
        lanes = slice((kv_head // 2) * LANES, (kv_head // 2 + 1) * LANES)
        kv_operands.append((_block_diag_pair(kw[:, lanes], kv_head % 2 == 1).astype(BF16),
                            _block_diag_pair(vw[:, lanes], kv_head % 2 == 1).astype(BF16)))

    def cols(head_a):
        return slice(head_a * ATTN_HEAD_DIM, (head_a + 2) * ATTN_HEAD_DIM)

    def scores(head_a):
        q2 = (q_ref[:, cols(head_a)] * (ATTN_HEAD_DIM ** -0.5)).astype(BF16)
        return lax.dot_general(q2, kv_operands[head_a // ATTN_GROUP][0], (((1,), (1,)), ((), ())),
                               preferred_element_type=F32)

    def finish(head_a, s2):
        probs, inv_den = [], []
        for t in range(2):
            head = head_a + t
            s_prev = s2[:, 2 * t * blk:(2 * t + 1) * blk]
            s_cur = s2[:, (2 * t + 1) * blk:(2 * t + 2) * blk]
            s = jnp.where(upper, s_prev, s_cur) - _alibi_slope(head) * dist
            sink = sink_ref[layer, head]
            mx = jnp.maximum(jnp.max(s, axis=-1, keepdims=True), sink)
            p = jnp.exp(s - mx)
            den = jnp.sum(p, axis=-1, keepdims=True) + jnp.exp(sink - mx)
            probs += [jnp.where(upper, p, 0.0), jnp.where(upper, 0.0, p)]
            inv_den.append(1.0 / den)
        p2 = jnp.concatenate(probs, axis=1).astype(BF16)
        o2 = jnp.dot(p2, kv_operands[head_a // ATTN_GROUP][1], preferred_element_type=F32)
        scale = jnp.where(out_lane < ATTN_HEAD_DIM, inv_den[0], inv_den[1])
        o_ref[:, cols(head_a)] = (o2 * scale).astype(o_ref.dtype)

    pair_heads = list(range(0, ATTN_Q_HEADS, 2))
    pending = [scores(hd) for hd in pair_heads[:SWA_LOOKAHEAD]]
    for i, head_a in enumerate(pair_heads):
        if i + SWA_LOOKAHEAD < len(pair_heads):
            pending.append(scores(pair_heads[i + SWA_LOOKAHEAD]))
        finish(head_a, pending.pop(0))


def _swa(u, sinks, layer, *, batch, seq):
    nb = seq // ATTN_BLOCK
    kcol = OFF_AK // ATTN_KV_WIDTH
    vcol = OFF_AV // ATTN_KV_WIDTH

    def cur(col):
        return lambda b, n: (b * nb + n, col)

    def prev(col):
        return lambda b, n: (b * nb + jnp.maximum(n - 1, 0), col)

    kv_shape = (ATTN_BLOCK, ATTN_KV_WIDTH)
    return pl.pallas_call(
        functools.partial(_swa_kernel, layer=layer),
        grid=(batch, nb),
        in_specs=[pl.BlockSpec(memory_space=pltpu.SMEM),
                  pl.BlockSpec((ATTN_BLOCK, ATTN_WIDTH), lambda b, n: (b * nb + n, 0)),
                  pl.BlockSpec(kv_shape, prev(kcol)), pl.BlockSpec(kv_shape, cur(kcol)),
                  pl.BlockSpec(kv_shape, prev(vcol)), pl.BlockSpec(kv_shape, cur(vcol))],
        out_specs=pl.BlockSpec((ATTN_BLOCK, ATTN_WIDTH), lambda b, n: (b * nb + n, 0)),
        out_shape=jax.ShapeDtypeStruct((batch * seq, ATTN_WIDTH), BF16),
        compiler_params=_params(("parallel", "parallel")),
        name="swa",
    )(sinks, u, u, u, u, u)


def _dn_gates_kernel(a_ref, wt_ref, alog_ref, bias_ref, gb_ref, gt_ref):
    raw = lax.dot_general(a_ref[...], wt_ref[...].astype(BF16), (((1,), (1,)), ((), ())),
                          preferred_element_type=F32)
    rows = raw.shape[0]
    lane = lax.broadcasted_iota(jnp.int32, raw.shape, 1)
    raw = jnp.where(lane < N_GATE_COLS, raw, 0.0)
    g = -jnp.exp(alog_ref[...]) * jax.nn.softplus(raw + bias_ref[...])
    pos = lax.broadcasted_iota(jnp.int32, raw.shape, 0) % DN_CHUNK
    acc = g
    shift = 1
    while shift < DN_CHUNK:
        acc = acc + jnp.where(pos >= shift, pltpu.roll(acc, shift, 0), 0.0)
        shift *= 2
    gb = jnp.where(lane < DN_HEADS, acc, jnp.where(lane < N_GATE_COLS, jax.nn.sigmoid(raw), 0.0))
    gb_ref[...] = gb
    gt_ref[...] = jnp.swapaxes(gb.reshape(rows // DN_CHUNK, DN_CHUNK, LANES), 1, 2)


def _dn_gates(a, w_in_t, alog, bias, layer, *, rows=512):
    m, d = a.shape
    rows = min(rows, m)
    nc = rows // DN_CHUNK
    return pl.pallas_call(
        _dn_gates_kernel,
        grid=(m // rows,),
        in_specs=[pl.BlockSpec((rows, d), lambda i: (i, 0)),
                  pl.BlockSpec((None, LANES, d), lambda i: (layer, OFF_DA // LANES, 0)),
                  pl.BlockSpec((None, 1, LANES), lambda i: (layer, 0, 0)),
                  pl.BlockSpec((None, 1, LANES), lambda i: (layer, 0, 0))],
        out_specs=[pl.BlockSpec((rows, LANES), lambda i: (i, 0)),
                   pl.BlockSpec((nc, LANES, DN_CHUNK), lambda i: (i, 0, 0))],
        out_shape=[jax.ShapeDtypeStruct((m, LANES), F32),
                   jax.ShapeDtypeStruct((m // DN_CHUNK, LANES, DN_CHUNK), F32)],
        compiler_params=_params(("parallel",)),
        name="dn_gates",
    )(a, w_in_t, alog, bias)


def _bdot(eq, a, b):
    return jnp.einsum(eq, a.astype(BF16), b.astype(BF16), preferred_element_type=F32)


def _unit_lower_inverse(low, ri, ci):
    base = SUBLANES
    same = lambda b: (ri & -b) == (ci & -b)
    x = -jnp.where(same(base), low, 0.0)
    inv = (ri == ci).astype(F32) + x
    size = 2
    while size < base:
        x = _bdot('cij,cjk->cik', x, x)
        inv = inv + _bdot('cij,cjk->cik', inv, x)
        size *= 2
    blk = base
    while blk < DN_CHUNK:
        off = jnp.where(same(2 * blk) & ~same(blk), low, 0.0)
        inv = inv - _bdot('cij,cjk->cik', inv, _bdot('cij,cjk->cik', off, inv))
        blk *= 2
    return inv


def _dn_kernel(xq_ref, xk_ref, xv_ref, z_ref, gb_ref, gt_ref, cwq_ref, cwk_ref, cwv_ref, onorm_ref,
               o_ref, state_ref, tail_ref, *, heads_per_step, rows):
    hg = pl.program_id(1)
    t = pl.program_id(2)
    nc = rows // DN_CHUNK
    dh = DN_HEAD_DIM

    @pl.when(t == 0)
    def _():
        state_ref[...] = jnp.zeros_like(state_ref)
        tail_ref[...] = jnp.zeros_like(tail_ref)

    def conv_silu(x_ref, cw_ref, idx):
        x = x_ref[...]
        ext = jnp.concatenate([tail_ref[idx], x], axis=0)
        w = cw_ref[...]
        y = x * w[CONV_WIDTH - 1:CONV_WIDTH]
        for s in range(1, CONV_WIDTH):
            y = y + pltpu.roll(ext, s, 0)[SUBLANES:] * w[CONV_WIDTH - 1 - s:CONV_WIDTH - s]
        tail_ref[idx] = x[rows - SUBLANES:]
        return y * jax.nn.sigmoid(y)

    yq = conv_silu(xq_ref, cwq_ref, 0)
    yk = conv_silu(xk_ref, cwk_ref, 1)
    yv = conv_silu(xv_ref, cwv_ref, 2)

    gball = gb_ref[...]
    lane = lax.broadcasted_iota(jnp.int32, gball.shape, 1)
    ri = lax.broadcasted_iota(jnp.int32, (DN_CHUNK, DN_CHUNK), 0)
    ci = lax.broadcasted_iota(jnp.int32, (DN_CHUNK, DN_CHUNK), 1)
    causal = ri >= ci
    strict = ri > ci

    def per_head(fn):
        return jnp.concatenate([fn(hh) for hh in range(heads_per_step)], axis=0)

    def head_tile(y, hh):
        return y[:, hh * dh:(hh + 1) * dh].reshape(nc, DN_CHUNK, dh)

    def gate_column(hh, offset):
        sel = lane == hg * heads_per_step + hh + offset
        return jnp.sum(jnp.where(sel, gball, 0.0), axis=-1, keepdims=True).reshape(nc, DN_CHUNK, 1)

    q3 = per_head(lambda hh: head_tile(yq, hh))
    k3 = per_head(lambda hh: head_tile(yk, hh))
    v3 = per_head(lambda hh: head_tile(yv, hh))
    q3 = q3 * lax.rsqrt(jnp.sum(q3 * q3, axis=-1, keepdims=True) + 1e-6) * (dh ** -0.5)
    k3 = k3 * lax.rsqrt(jnp.sum(k3 * k3, axis=-1, keepdims=True) + 1e-6)
    gc3 = per_head(lambda hh: gate_column(hh, 0))
    b3 = per_head(lambda hh: gate_column(hh, DN_HEADS))
    grow = per_head(lambda hh: gt_ref[:, pl.ds(hg * heads_per_step + hh, 1), :])
    decay = jnp.exp(jnp.where(causal, gc3 - grow, -jnp.inf))
    eg = jnp.exp(gc3)
    gl = gc3[:, DN_CHUNK - 1:DN_CHUNK, :]
    kb = k3 * b3
    kk = _bdot('cid,cjd->cij', kb, k3)
    qk = _bdot('cid,cjd->cij', q3, k3)
    low = jnp.where(strict, kk * decay, 0.0)
    a_intra = qk * decay
    tinv = _unit_lower_inverse(low, ri, ci)
    rhs = jnp.concatenate([kb * eg, v3 * b3], axis=-1)
    wu = _bdot('cij,cjd->cid', tinv, rhs)
    kd = k3 * jnp.exp(gl - gc3)
    pn = _bdot('ncd,nce->nde', kd, wu)
    aq = _bdot('cij,cjd->cid', a_intra, wu)
    qeff = q3 * eg - aq[..., :dh]
    neg_p_q = jnp.concatenate([-pn[..., :dh], qeff], axis=1).astype(BF16)
    n_mat = pn[..., dh:]
    o_loc = aq[..., dh:]
    g_last = jnp.exp(gl)

    states = [state_ref[hh] for hh in range(heads_per_step)]
    outs = [[] for _ in range(heads_per_step)]
    for cidx in range(nc):
        prods = [jnp.dot(neg_p_q[hh * nc + cidx], states[hh].astype(BF16), preferred_element_type=F32)
                 for hh in range(heads_per_step)]
        for hh in range(heads_per_step):
            b = hh * nc + cidx
            outs[hh].append(prods[hh][dh:] + o_loc[b])
            states[hh] = g_last[b] * states[hh] + prods[hh][:dh] + n_mat[b]
    for hh in range(heads_per_step):
        state_ref[hh] = states[hh]
        cols = slice(hh * dh, (hh + 1) * dh)
        o = jnp.concatenate(outs[hh], axis=0)
        z = z_ref[:, cols]
        o_ref[:, cols] = (_rms(o, onorm_ref[...]) * (z * jax.nn.sigmoid(z))).astype(o_ref.dtype)


def _deltanet(u, gb, gt, conv_w, out_norm, layer, *, batch, seq, heads_per_step=4, rows=512):
    wcols = heads_per_step * DN_HEAD_DIM
    rows = min(rows, seq)
    nt = seq // rows
    nc = rows // DN_CHUNK
    n_hg = DN_HEADS // heads_per_step
    qoff = OFF_DQKV // wcols
    koff = (OFF_DQKV + DN_WIDTH) // wcols
    voff = (OFF_DQKV + 2 * DN_WIDTH) // wcols
    zoff = OFF_DZ // wcols

    def xspec(off):
        return pl.BlockSpec((rows, wcols), lambda b, g, t: (b * nt + t, off + g))

    def cwspec(off):
        return pl.BlockSpec((None, CONV_WIDTH, wcols), lambda b, g, t: (layer, 0, off + g))

    return pl.pallas_call(
        functools.partial(_dn_kernel, heads_per_step=heads_per_step, rows=rows),
        grid=(batch, n_hg, nt),
        in_specs=[xspec(qoff), xspec(koff), xspec(voff), xspec(zoff),
                  pl.BlockSpec((rows, LANES), lambda b, g, t: (b * nt + t, 0)),
                  pl.BlockSpec((nc, LANES, DN_CHUNK), lambda b, g, t: (b * nt + t, 0, 0)),
                  cwspec(0), cwspec(DN_WIDTH // wcols), cwspec(2 * DN_WIDTH // wcols),
                  pl.BlockSpec((None, 1, DN_HEAD_DIM), lambda b, g, t: (layer, 0, 0))],
        out_specs=pl.BlockSpec((rows, wcols), lambda b, g, t: (b * nt + t, g)),
        out_shape=jax.ShapeDtypeStruct((batch * seq, DN_WIDTH), BF16),
        scratch_shapes=[pltpu.VMEM((heads_per_step, DN_HEAD_DIM, DN_HEAD_DIM), F32),
                        pltpu.VMEM((3, SUBLANES, wcols), F32)],
        compiler_params=_params(("parallel", "parallel", "arbitrary")),
        name="deltanet",
    )(u, u, u, u, gb, gt, conv_w, conv_w, conv_w, out_norm)


def _xattn_out_kernel(q_ref, kv_ref, wo_ref, h_ref, gp_ref, gn_ref, hn_ref, a_ref, wo_bf16_ref):
    @pl.when((pl.program_id(0) == 0) & (pl.program_id(1) == 0))
    def _():
        wo_bf16_ref[...] = wo_ref[...].astype(BF16)

    heads = []
    for h in range(X_HEADS):
        cols = slice(h * X_HEAD_DIM, (h + 1) * X_HEAD_DIM)
        k = kv_ref[:, cols]
        v = kv_ref[:, X_WIDTH + h * X_HEAD_DIM:X_WIDTH + (h + 1) * X_HEAD_DIM]
        s = lax.dot_general(q_ref[:, cols], k, (((1,), (1,)), ((), ())),
                            preferred_element_type=F32) * (X_HEAD_DIM ** -0.5)
        p = jnp.exp(s - jnp.max(s, axis=-1, keepdims=True))
        den = jnp.sum(p, axis=-1, keepdims=True)
        o = jnp.dot(p.astype(BF16), v, preferred_element_type=F32)
        heads.append((o / den).astype(BF16))
    y = jnp.dot(jnp.concatenate(heads, axis=1), wo_bf16_ref[...], preferred_element_type=F32)
    hn = h_ref[...] + _rms(y, gp_ref[...])
    hn_ref[...] = hn
    a_ref[...] = _rms(hn, gn_ref[...]).astype(a_ref.dtype)


def _xattn_out(q, kv, w_o, h, g_post, g_next, layer, *, batch, seq, rows=256):
    rows = min(rows, seq)
    nt = seq // rows
    d = h.shape[1]
    row_spec = pl.BlockSpec((rows, d), lambda b, t: (b * nt + t, 0))
    gain_spec = pl.BlockSpec((None, 1, d), lambda b, t: (layer, 0, 0))
    return pl.pallas_call(
        _xattn_out_kernel,
        grid=(batch, nt),
        in_specs=[pl.BlockSpec((rows, X_WIDTH), lambda b, t: (b * nt + t, 0)),
                  pl.BlockSpec((MEM_LEN, 2 * X_WIDTH), lambda b, t: (b, 0)),
                  pl.BlockSpec((None, X_WIDTH, d), lambda b, t: (layer, 0, 0)),
                  row_spec, gain_spec, gain_spec],
        out_specs=[row_spec, row_spec],
        out_shape=[jax.ShapeDtypeStruct((batch * seq, d), F32), jax.ShapeDtypeStruct((batch * seq, d), BF16)],
        scratch_shapes=[pltpu.VMEM((X_WIDTH, d), BF16)],
        compiler_params=_params(("arbitrary", "arbitrary")),
        name="xattn_out",
    )(q, kv, w_o, h, g_post, g_next)


def _pad_lanes(v):
    return jnp.pad(v, ((0, 0), (0, LANES - v.shape[1])))[:, None, :]


def kernel(x, mem, norm_mix_pre, w_in, attn_sinks, dn_conv_w, dn_a_log, dn_dt_bias, dn_out_norm,
           w_out, norm_mix_post, norm_mem, norm_x_pre, w_xq, w_xkv, w_xo, norm_x_post,
           norm_ffn_pre, w_gate_up, w_down, norm_ffn_post):
    batch, seq, d = x.shape
    tokens = batch * seq
    h = x.reshape(tokens, d)
    memf = mem.reshape(batch * MEM_LEN, d)
    gain = lambda g: g[:, None, :]
    g_mix_pre, g_mix_post, g_mem = gain(norm_mix_pre), gain(norm_mix_post), gain(norm_mem)
    g_x_pre, g_x_post = gain(norm_x_pre), gain(norm_x_post)
    g_ffn_pre, g_ffn_post = gain(norm_ffn_pre), gain(norm_ffn_post)
    g_dn_out = gain(dn_out_norm)
    alog = _pad_lanes(dn_a_log)
    dtb = _pad_lanes(dn_dt_bias)
    w_in_t = jnp.swapaxes(w_in, 1, 2)

    a = _norm_cast(h, g_mix_pre, 0, rows=256)
    for l in range(DEPTH):
        u = _matmul_nt(a, w_in_t, l, row_off_blocks=0, n_cols=IN_MAIN_COLS, tm=1024, tn=512,
                       out_dtype=F32, name="in_proj")
        attn = _swa(u, attn_sinks, l, batch=batch, seq=seq)
        gb, gt = _dn_gates(a, w_in_t, alog, dtb, l)
        dn = _deltanet(u, gb, gt, dn_conv_w, g_dn_out, l, batch=batch, seq=seq)
        mix = _matmul_cat(attn, dn, w_out, l, tm=1024, tn=512, out_dtype=BF16, name="out_proj")
        h, qx = _resnorm_proj(mix, h, g_mix_post, g_x_pre, w_xq, l)
        mem_n = _norm_cast(memf, g_mem, l, rows=256)
        kv = _matmul(mem_n, w_xkv, l, col_off_blocks=0, n_cols=2 * X_WIDTH, tm=batch * MEM_LEN, tn=512,
                     out_dtype=BF16, name="xkv_proj")
        h, a = _xattn_out(qx, kv, w_xo, h, g_x_post, g_ffn_pre, l, batch=batch, seq=seq)
        hid = _matmul_swiglu(a, w_gate_up, l, tm=2048, tn=256)
        ff = _matmul(hid, w_down, l, col_off_blocks=0, n_cols=d, tm=1024, tn=256,
                     out_dtype=BF16, name="ffn_down")
        if l + 1 < DEPTH:
            h, a = _resnorm(ff, h, g_ffn_post, l, g_mix_pre, l + 1)
        else:
            h, _ = _resnorm(ff, h, g_ffn_post, l, None, None)
    return h.reshape(batch, seq, d)
```

```python
import functools

import jax
import jax.numpy as jnp
from jax import lax
from jax.experimental import pallas as pl
from jax.experimental.pallas import tpu as pltpu

F32 = jnp.float32
BF16 = jnp.bfloat16

D_MODEL = 4096
DEPTH = 4
MEM_LEN = 256
ATTN_WIDTH = 2048
DN_WIDTH = 2048
ATTN_HEAD_DIM = 64
ATTN_Q_HEADS = 32
ATTN_KV_HEADS = 4
ATTN_GROUP = 8
ATTN_KV_WIDTH = 256
WINDOW = 128
ATTN_BLOCK = 128
DN_HEAD_DIM = 128
DN_HEADS = 16
CONV_WIDTH = 4
DN_CHUNK = 64
X_HEADS = 4
X_HEAD_DIM = 128
X_WIDTH = 512
FFN_HIDDEN = 11008
RMS_EPS = 1e-6

OFF_AQ = 0
OFF_AK = OFF_AQ + ATTN_WIDTH
OFF_AV = OFF_AK + ATTN_KV_WIDTH
OFF_DQKV = OFF_AV + ATTN_KV_WIDTH
OFF_DZ = OFF_DQKV + 3 * DN_WIDTH
OFF_DA = OFF_DZ + DN_WIDTH
IN_MAIN_COLS = OFF_DA
N_GATE_COLS = 2 * DN_HEADS

LANES = 128
SUBLANES = 8
VMEM_LIMIT_BYTES = 56 * 1024 * 1024


def _params(semantics, vmem=VMEM_LIMIT_BYTES):
    return pltpu.CompilerParams(dimension_semantics=semantics, vmem_limit_bytes=vmem)


def _rms(x, gain):
    ms = jnp.mean(x * x, axis=-1, keepdims=True)
    return x * lax.rsqrt(ms + RMS_EPS) * gain


def _norm_kernel(x_ref, g_ref, o_ref):
    o_ref[...] = _rms(x_ref[...], g_ref[...]).astype(o_ref.dtype)


def _norm_cast(x, gains, layer, *, rows):
    m, d = x.shape
    rows = min(rows, m)
    return pl.pallas_call(
        _norm_kernel,
        grid=(m // rows,),
        in_specs=[pl.BlockSpec((rows, d), lambda i: (i, 0)),
                  pl.BlockSpec((None, 1, d), lambda i: (layer, 0, 0))],
        out_specs=pl.BlockSpec((rows, d), lambda i: (i, 0)),
        out_shape=jax.ShapeDtypeStruct((m, d), BF16),
        compiler_params=_params(("parallel",)),
        name="norm_cast",
    )(x, gains)


def _resnorm_kernel(y_ref, h_ref, gp_ref, gn_ref, hn_ref, a_ref):
    hn = h_ref[...] + _rms(y_ref[...].astype(F32), gp_ref[...])
    hn_ref[...] = hn
    a_ref[...] = _rms(hn, gn_ref[...]).astype(a_ref.dtype)


def _resnorm_proj_kernel(y_ref, h_ref, gp_ref, gn_ref, w_ref, hn_ref, q_ref, w_bf16_ref):
    @pl.when(pl.program_id(0) == 0)
    def _():
        w_bf16_ref[...] = w_ref[...].astype(BF16)

    hn = h_ref[...] + _rms(y_ref[...].astype(F32), gp_ref[...])
    hn_ref[...] = hn
    a = _rms(hn, gn_ref[...]).astype(BF16)
    q_ref[...] = jnp.dot(a, w_bf16_ref[...], preferred_element_type=F32).astype(q_ref.dtype)


def _resnorm_proj(y, h, g_post, g_next, w, layer, *, rows=256):
    m, d = y.shape
    n = w.shape[2]
    rows = min(rows, m)
    row_spec = pl.BlockSpec((rows, d), lambda i: (i, 0))
    gain_spec = pl.BlockSpec((None, 1, d), lambda i: (layer, 0, 0))
    return pl.pallas_call(
        _resnorm_proj_kernel,
        grid=(m // rows,),
        in_specs=[row_spec, row_spec, gain_spec, gain_spec,
                  pl.BlockSpec((None, d, n), lambda i: (layer, 0, 0))],
        out_specs=[row_spec, pl.BlockSpec((rows, n), lambda i: (i, 0))],
        out_shape=[jax.ShapeDtypeStruct((m, d), F32), jax.ShapeDtypeStruct((m, n), BF16)],
        scratch_shapes=[pltpu.VMEM((d, n), BF16)],
        compiler_params=_params(("arbitrary",)),
        name="resnorm_xq",
    )(y, h, g_post, g_next, w)


def _resnorm_last_kernel(y_ref, h_ref, gp_ref, hn_ref):
    hn_ref[...] = h_ref[...] + _rms(y_ref[...].astype(F32), gp_ref[...])


def _resnorm(y, h, g_post, l_post, g_next, l_next, *, rows=256):
    m, d = y.shape
    rows = min(rows, m)
    row_spec = pl.BlockSpec((rows, d), lambda i: (i, 0))
    if g_next is None:
        return pl.pallas_call(
            _resnorm_last_kernel,
            grid=(m // rows,),
            in_specs=[row_spec, row_spec, pl.BlockSpec((None, 1, d), lambda i: (l_post, 0, 0))],
            out_specs=row_spec,
            out_shape=jax.ShapeDtypeStruct((m, d), F32),
            compiler_params=_params(("parallel",)),
            name="resnorm_last",
        )(y, h, g_post), None
    return pl.pallas_call(
        _resnorm_kernel,
        grid=(m // rows,),
        in_specs=[row_spec, row_spec,
                  pl.BlockSpec((None, 1, d), lambda i: (l_post, 0, 0)),
                  pl.BlockSpec((None, 1, d), lambda i: (l_next, 0, 0))],
        out_specs=[row_spec, row_spec],
        out_shape=[jax.ShapeDtypeStruct((m, d), F32), jax.ShapeDtypeStruct((m, d), BF16)],
        compiler_params=_params(("parallel",)),
        name="resnorm",
    )(y, h, g_post, g_next)


ROW_TILE_PREFETCH_BYTES = 16 * 1024 * 1024


def _row_resident_spec(tm, k):
    if 2 * tm * k * 2 <= ROW_TILE_PREFETCH_BYTES:
        return pl.BlockSpec((tm, k), lambda i, j: (i, 0))
    return pl.BlockSpec((tm, k), lambda i, j: (i, 0), pipeline_mode=pl.Buffered(1))


def _mm_kernel(x_ref, w_ref, o_ref):
    acc = jnp.dot(x_ref[...], w_ref[...].astype(BF16), preferred_element_type=F32)
    o_ref[...] = acc.astype(o_ref.dtype)


def _matmul(x, w, layer, *, col_off_blocks, n_cols, tm, tn, out_dtype, name):
    m, k = x.shape
    tm = min(tm, m)
    assert w.shape[1] == k and m % tm == 0 and n_cols % tn == 0
    assert (col_off_blocks * tn + n_cols) <= w.shape[2]
    return pl.pallas_call(
        _mm_kernel,
        grid=(m // tm, n_cols // tn),
        in_specs=[_row_resident_spec(tm, k),
                  pl.BlockSpec((None, k, tn), lambda i, j: (layer, 0, j + col_off_blocks))],
        out_specs=pl.BlockSpec((tm, tn), lambda i, j: (i, j)),
        out_shape=jax.ShapeDtypeStruct((m, n_cols), out_dtype),
        compiler_params=_params(("parallel", "arbitrary")),
        name=name,
    )(x, w)


def _mm_nt_kernel(x_ref, wt_ref, o_ref):
    acc = lax.dot_general(x_ref[...], wt_ref[...].astype(BF16), (((1,), (1,)), ((), ())),
                          preferred_element_type=F32)
    o_ref[...] = acc.astype(o_ref.dtype)


def _matmul_nt(x, wt, layer, *, row_off_blocks, n_cols, tm, tn, out_dtype, name):
    m, k = x.shape
    tm = min(tm, m)
    assert wt.shape[2] == k and m % tm == 0 and n_cols % tn == 0
    assert (row_off_blocks * tn + n_cols) <= wt.shape[1]
    return pl.pallas_call(
        _mm_nt_kernel,
        grid=(m // tm, n_cols // tn),
        in_specs=[_row_resident_spec(tm, k),
                  pl.BlockSpec((None, tn, k), lambda i, j: (layer, j + row_off_blocks, 0))],
        out_specs=pl.BlockSpec((tm, tn), lambda i, j: (i, j)),
        out_shape=jax.ShapeDtypeStruct((m, n_cols), out_dtype),
        compiler_params=_params(("parallel", "arbitrary")),
        name=name,
    )(x, wt)


def _mm2_kernel(x1_ref, x2_ref, w_ref, o_ref):
    k1 = x1_ref.shape[1]
    w = w_ref[...].astype(BF16)
    acc = jnp.dot(x1_ref[...], w[:k1], preferred_element_type=F32)
    acc = acc + jnp.dot(x2_ref[...], w[k1:], preferred_element_type=F32)
    o_ref[...] = acc.astype(o_ref.dtype)


def _matmul_cat(x1, x2, w, layer, *, tm, tn, out_dtype, name):
    m, k1 = x1.shape
    k2 = x2.shape[1]
    n = w.shape[2]
    tm = min(tm, m)
    assert w.shape[1] == k1 + k2 and m % tm == 0 and n % tn == 0
    return pl.pallas_call(
        _mm2_kernel,
        grid=(m // tm, n // tn),
        in_specs=[_row_resident_spec(tm, k1), _row_resident_spec(tm, k2),
                  pl.BlockSpec((None, k1 + k2, tn), lambda i, j: (layer, 0, j))],
        out_specs=pl.BlockSpec((tm, tn), lambda i, j: (i, j)),
        out_shape=jax.ShapeDtypeStruct((m, n), out_dtype),
        compiler_params=_params(("parallel", "arbitrary")),
        name=name,
    )(x1, x2, w)


def _swiglu_kernel(x_ref, wg_ref, wu_ref, o_ref):
    x = x_ref[...]
    g = jnp.dot(x, wg_ref[...].astype(BF16), preferred_element_type=F32)
    u = jnp.dot(x, wu_ref[...].astype(BF16), preferred_element_type=F32)
    o_ref[...] = (g * jax.nn.sigmoid(g) * u).astype(o_ref.dtype)


def _matmul_swiglu(x, w, layer, *, tm, tn):
    m, k = x.shape
    tm = min(tm, m)
    hid = w.shape[2] // 2
    assert hid % tn == 0 and m % tm == 0
    up_off = hid // tn
    return pl.pallas_call(
        _swiglu_kernel,
        grid=(m // tm, hid // tn),
        in_specs=[_row_resident_spec(tm, k),
                  pl.BlockSpec((None, k, tn), lambda i, j: (layer, 0, j)),
                  pl.BlockSpec((None, k, tn), lambda i, j: (layer, 0, j + up_off))],
        out_specs=pl.BlockSpec((tm, tn), lambda i, j: (i, j)),
        out_shape=jax.ShapeDtypeStruct((m, hid), BF16),
        compiler_params=_params(("parallel", "arbitrary")),
        name="ffn_gate_up",
    )(x, w, w)


SWA_LOOKAHEAD = 3


def _alibi_slope(head):
    return 2.0 ** (-8.0 * (head + 1) / ATTN_Q_HEADS)


def _block_diag_pair(blk, use_right):
    lane = lax.broadcasted_iota(jnp.int32, blk.shape, 1)
    left = lane < ATTN_HEAD_DIM
    rolled = pltpu.roll(blk, ATTN_HEAD_DIM, 1)
    if use_right:
        top, bot = jnp.where(left, rolled, 0.0), jnp.where(left, 0.0, blk)
    else:
        top, bot = jnp.where(left, blk, 0.0), jnp.where(left, 0.0, rolled)
    return jnp.concatenate([top, bot], axis=0)


def _swa_kernel(sink_ref, q_ref, kp_ref, kc_ref, vp_ref, vc_ref, o_ref, *, layer):
    n = pl.program_id(1)
    blk = ATTN_BLOCK
    kw = jnp.concatenate([kp_ref[...], kc_ref[...]], axis=0)
    vw = jnp.concatenate([vp_ref[...], vc_ref[...]], axis=0)
    r = lax.broadcasted_iota(jnp.int32, (blk, blk), 0)
    c = lax.broadcasted_iota(jnp.int32, (blk, blk), 1)
    upper = c > r
    dist = jnp.where(upper, r + blk - c, r - c).astype(F32)
    dist = jnp.where(upper & (n == 0), jnp.inf, dist)
    out_lane = lax.broadcasted_iota(jnp.int32, (blk, LANES), 1)

    kv_operands = []
    for kv_head in range(ATTN_KV_HEADS):
        lanes = slice((kv_head // 2) * LANES, (kv_head // 2 + 1) * LANES)
        kv_operands.append((_block_diag_pair(kw[:, lanes], kv_head % 2 == 1).astype(BF16),
                            _block_diag_pair(vw[:, lanes], kv_head % 2 == 1).astype(BF16)))

    def cols(head_a):
        return slice(head_a * ATTN_HEAD_DIM, (head_a + 2) * ATTN_HEAD_DIM)

    def scores(head_a):
        q2 = (q_ref[:, cols(head_a)] * (ATTN_HEAD_DIM ** -0.5)).astype(BF16)
        return lax.dot_general(q2, kv_operands[head_a // ATTN_GROUP][0], (((1,), (1,)), ((), ())),
                               preferred_element_type=F32)

    def finish(head_a, s2):
        probs, inv_den = [], []
        for t in range(2):
            head = head_a + t
            s_prev = s2[:, 2 * t * blk:(2 * t + 1) * blk]
            s_cur = s2[:, (2 * t + 1) * blk:(2 * t + 2) * blk]
            s = jnp.where(upper, s_prev, s_cur) - _alibi_slope(head) * dist
            sink = sink_ref[layer, head]
            mx = jnp.maximum(jnp.max(s, axis=-1, keepdims=True), sink)
            p = jnp.exp(s - mx)
            den = jnp.sum(p, axis=-1, keepdims=True) + jnp.exp(sink - mx)
            probs += [jnp.where(upper, p, 0.0), jnp.where(upper, 0.0, p)]
            inv_den.append(1.0 / den)
        p2 = jnp.concatenate(probs, axis=1).astype(BF16)
        o2 = jnp.dot(p2, kv_operands[head_a // ATTN_GROUP][1], preferred_element_type=F32)
        scale = jnp.where(out_lane < ATTN_HEAD_DIM, inv_den[0], inv_den[1])
        o_ref[:, cols(head_a)] = (o2 * scale).astype(o_ref.dtype)

    pair_heads = list(range(0, ATTN_Q_HEADS, 2))
    pending = [scores(hd) for hd in pair_heads[:SWA_LOOKAHEAD]]
    for i, head_a in enumerate(pair_heads):
        if i + SWA_LOOKAHEAD < len(pair_heads):
            pending.append(scores(pair_heads[i + SWA_LOOKAHEAD]))
        finish(head_a, pending.pop(0))


def _swa(u, sinks, layer, *, batch, seq):
    nb = seq // ATTN_BLOCK
    kcol = OFF_AK // ATTN_KV_WIDTH
    vcol = OFF_AV // ATTN_KV_WIDTH

    def cur(col):
        return lambda b, n: (b * nb + n, col)

    def prev(col):
        return lambda b, n: (b * nb + jnp.maximum(n - 1, 0), col)

    kv_shape = (ATTN_BLOCK, ATTN_KV_WIDTH)
    return pl.pallas_call(
        functools.partial(_swa_kernel, layer=layer),
        grid=(batch, nb),
        in_specs=[pl.BlockSpec(memory_space=pltpu.SMEM),
                  pl.BlockSpec((ATTN_BLOCK, ATTN_WIDTH), lambda b, n: (b * nb + n, 0)),
                  pl.BlockSpec(kv_shape, prev(kcol)), pl.BlockSpec(kv_shape, cur(kcol)),
                  pl.BlockSpec(kv_shape, prev(vcol)), pl.BlockSpec(kv_shape, cur(vcol))],
        out_specs=pl.BlockSpec((ATTN_BLOCK, ATTN_WIDTH), lambda b, n: (b * nb + n, 0)),
        out_shape=jax.ShapeDtypeStruct((batch * seq, ATTN_WIDTH), BF16),
        compiler_params=_params(("parallel", "parallel")),
        name="swa",
    )(sinks, u, u, u, u, u)


def _dn_gates_kernel(a_ref, wt_ref, alog_ref, bias_ref, gb_ref, gt_ref):
    raw = lax.dot_general(a_ref[...], wt_ref[...].astype(BF16), (((1,), (1,)), ((), ())),
                          preferred_element_type=F32)
    rows = raw.shape[0]
    lane = lax.broadcasted_iota(jnp.int32, raw.shape, 1)
    raw = jnp.where(lane < N_GATE_COLS, raw, 0.0)
    g = -jnp.exp(alog_ref[...]) * jax.nn.softplus(raw + bias_ref[...])
    pos = lax.broadcasted_iota(jnp.int32, raw.shape, 0) % DN_CHUNK
    acc = g
    shift = 1
    while shift < DN_CHUNK:
        acc = acc + jnp.where(pos >= shift, pltpu.roll(acc, shift, 0), 0.0)
        shift *= 2
    gb = jnp.where(lane < DN_HEADS, acc, jnp.where(lane < N_GATE_COLS, jax.nn.sigmoid(raw), 0.0))
    gb_ref[...] = gb
    gt_ref[...] = jnp.swapaxes(gb.reshape(rows // DN_CHUNK, DN_CHUNK, LANES), 1, 2)


def _dn_gates(a, w_in_t, alog, bias, layer, *, rows=512):
    m, d = a.shape
    rows = min(rows, m)
    nc = rows // DN_CHUNK
    return pl.pallas_call(
        _dn_gates_kernel,
        grid=(m // rows,),
        in_specs=[pl.BlockSpec((rows, d), lambda i: (i, 0)),
                  pl.BlockSpec((None, LANES, d), lambda i: (layer, OFF_DA // LANES, 0)),
                  pl.BlockSpec((None, 1, LANES), lambda i: (layer, 0, 0)),
                  pl.BlockSpec((None, 1, LANES), lambda i: (layer, 0, 0))],
        out_specs=[pl.BlockSpec((rows, LANES), lambda i: (i, 0)),
                   pl.BlockSpec((nc, LANES, DN_CHUNK), lambda i: (i, 0, 0))],
        out_shape=[jax.ShapeDtypeStruct((m, LANES), F32),
                   jax.ShapeDtypeStruct((m // DN_CHUNK, LANES, DN_CHUNK), F32)],
        compiler_params=_params(("parallel",)),
        name="dn_gates",
    )(a, w_in_t, alog, bias)


def _bdot(eq, a, b):
    return jnp.einsum(eq, a.astype(BF16), b.astype(BF16), preferred_element_type=F32)


def _unit_lower_inverse(low, ri, ci):
    base = SUBLANES
    same = lambda b: (ri & -b) == (ci & -b)
    x = -jnp.where(same(base), low, 0.0)
    inv = (ri == ci).astype(F32) + x
    size = 2
    while size < base:
        x = _bdot('cij,cjk->cik', x, x)
        inv = inv + _bdot('cij,cjk->cik', inv, x)
        size *= 2
    blk = base
    while blk < DN_CHUNK:
        off = jnp.where(same(2 * blk) & ~same(blk), low, 0.0)
        inv = inv - _bdot('cij,cjk->cik', inv, _bdot('cij,cjk->cik', off, inv))
        blk *= 2
    return inv


def _dn_kernel(xq_ref, xk_ref, xv_ref, z_ref, gb_ref, gt_ref, cwq_ref, cwk_ref, cwv_ref, onorm_ref,
               o_ref, state_ref, tail_ref, *, heads_per_step, rows):
    hg = pl.program_id(1)
    t = pl.program_id(2)
    nc = rows // DN_CHUNK
    dh = DN_HEAD_DIM

    @pl.when(t == 0)
    def _():
        state_ref[...] = jnp.zeros_like(state_ref)
        tail_ref[...] = jnp.zeros_like(tail_ref)

    def conv_silu(x_ref, cw_ref, idx):
        x = x_ref[...]
        ext = jnp.concatenate([tail_ref[idx], x], axis=0)
        w = cw_ref[...]
        y = x * w[CONV_WIDTH - 1:CONV_WIDTH]
        for s in range(1, CONV_WIDTH):
            y = y + pltpu.roll(ext, s, 0)[SUBLANES:] * w[CONV_WIDTH - 1 - s:CONV_WIDTH - s]
        tail_ref[idx] = x[rows - SUBLANES:]
        return y * jax.nn.sigmoid(y)

    yq = conv_silu(xq_ref, cwq_ref, 0)
    yk = conv_silu(xk_ref, cwk_ref, 1)
    yv = conv_silu(xv_ref, cwv_ref, 2)

    gball = gb_ref[...]
    lane = lax.broadcasted_iota(jnp.int32, gball.shape, 1)
    ri = lax.broadcasted_iota(jnp.int32, (DN_CHUNK, DN_CHUNK), 0)
    ci = lax.broadcasted_iota(jnp.int32, (DN_CHUNK, DN_CHUNK), 1)
    causal = ri >= ci
    strict = ri > ci

    def per_head(fn):
        return jnp.concatenate([fn(hh) for hh in range(heads_per_step)], axis=0)

    def head_tile(y, hh):
        return y[:, hh * dh:(hh + 1) * dh].reshape(nc, DN_CHUNK, dh)

    def gate_column(hh, offset):
        sel = lane == hg * heads_per_step + hh + offset
        return jnp.sum(jnp.where(sel, gball, 0.0), axis=-1, keepdims=True).reshape(nc, DN_CHUNK, 1)

    q3 = per_head(lambda hh: head_tile(yq, hh))
    k3 = per_head(lambda hh: head_tile(yk, hh))
    v3 = per_head(lambda hh: head_tile(yv, hh))
    q3 = q3 * lax.rsqrt(jnp.sum(q3 * q3, axis=-1, keepdims=True) + 1e-6) * (dh ** -0.5)
    k3 = k3 * lax.rsqrt(jnp.sum(k3 * k3, axis=-1, keepdims=True) + 1e-6)
    gc3 = per_head(lambda hh: gate_column(hh, 0))
    b3 = per_head(lambda hh: gate_column(hh, DN_HEADS))
    grow = per_head(lambda hh: gt_ref[:, pl.ds(hg * heads_per_step + hh, 1), :])
    decay = jnp.exp(jnp.where(causal, gc3 - grow, -jnp.inf))
    eg = jnp.exp(gc3)
    gl = gc3[:, DN_CHUNK - 1:DN_CHUNK, :]
    kb = k3 * b3
    kk = _bdot('cid,cjd->cij', kb, k3)
    qk = _bdot('cid,cjd->cij', q3, k3)
    low = jnp.where(strict, kk * decay, 0.0)
    a_intra = qk * decay
    tinv = _unit_lower_inverse(low, ri, ci)
    rhs = jnp.concatenate([kb * eg, v3 * b3], axis=-1)
    wu = _bdot('cij,cjd->cid', tinv, rhs)
    kd = k3 * jnp.exp(gl - gc3)
    pn = _bdot('ncd,nce->nde', kd, wu)
    aq = _bdot('cij,cjd->cid', a_intra, wu)
    qeff = q3 * eg - aq[..., :dh]
    neg_p_q = jnp.concatenate([-pn[..., :dh], qeff], axis=1).astype(BF16)
    n_mat = pn[..., dh:]
    o_loc = aq[..., dh:]
    g_last = jnp.exp(gl)

    states = [state_ref[hh] for hh in range(heads_per_step)]
    outs = [[] for _ in range(heads_per_step)]
    for cidx in range(nc):
        prods = [jnp.dot(neg_p_q[hh * nc + cidx], states[hh].astype(BF16), preferred_element_type=F32)
                 for hh in range(heads_per_step)]
        for hh in range(heads_per_step):
            b = hh * nc + cidx
            outs[hh].append(prods[hh][dh:] + o_loc[b])
            states[hh] = g_last[b] * states[hh] + prods[hh][:dh] + n_mat[b]
    for hh in range(heads_per_step):
        state_ref[hh] = states[hh]
        cols = slice(hh * dh, (hh + 1) * dh)
        o = jnp.concatenate(outs[hh], axis=0)
        z = z_ref[:, cols]
        o_ref[:, cols] = (_rms(o, onorm_ref[...]) * (z * jax.nn.sigmoid(z))).astype(o_ref.dtype)


def _deltanet(u, gb, gt, conv_w, out_norm, layer, *, batch, seq, heads_per_step=4, rows=512):
    wcols = heads_per_step * DN_HEAD_DIM
    rows = min(rows, seq)
    nt = seq // rows
    nc = rows // DN_CHUNK
    n_hg = DN_HEADS // heads_per_step
    qoff = OFF_DQKV // wcols
    koff = (OFF_DQKV + DN_WIDTH) // wcols
    voff = (OFF_DQKV + 2 * DN_WIDTH) // wcols
    zoff = OFF_DZ // wcols

    def xspec(off):
        return pl.BlockSpec((rows, wcols), lambda b, g, t: (b * nt + t, off + g))

    def cwspec(off):
        return pl.BlockSpec((None, CONV_WIDTH, wcols), lambda b, g, t: (layer, 0, off + g))

    return pl.pallas_call(
        functools.partial(_dn_kernel, heads_per_step=heads_per_step, rows=rows),
        grid=(batch, n_hg, nt),
        in_specs=[xspec(qoff), xspec(koff), xspec(voff), xspec(zoff),
                  pl.BlockSpec((rows, LANES), lambda b, g, t: (b * nt + t, 0)),
                  pl.BlockSpec((nc, LANES, DN_CHUNK), lambda b, g, t: (b * nt + t, 0, 0)),
                  cwspec(0), cwspec(DN_WIDTH // wcols), cwspec(2 * DN_WIDTH // wcols),
                  pl.BlockSpec((None, 1, DN_HEAD_DIM), lambda b, g, t: (layer, 0, 0))],
        out_specs=pl.BlockSpec((rows, wcols), lambda b, g, t: (b * nt + t, g)),
        out_shape=jax.ShapeDtypeStruct((batch * seq, DN_WIDTH), BF16),
        scratch_shapes=[pltpu.VMEM((heads_per_step, DN_HEAD_DIM, DN_HEAD_DIM), F32),
                        pltpu.VMEM((3, SUBLANES, wcols), F32)],
        compiler_params=_params(("parallel", "parallel", "arbitrary")),
        name="deltanet",
    )(u, u, u, u, gb, gt, conv_w, conv_w, conv_w, out_norm)


def _xattn_out_kernel(q_ref, kv_ref, wo_ref, h_ref, gp_ref, gn_ref, hn_ref, a_ref, wo_bf16_ref):
    @pl.when((pl.program_id(0) == 0) & (pl.program_id(1) == 0))
    def _():
        wo_bf16_ref[...] = wo_ref[...].astype(BF16)

    def cols(h, off=0):
        return slice(off + h * X_HEAD_DIM, off + (h + 1) * X_HEAD_DIM)

    scores = [lax.dot_general(q_ref[:, cols(h)], kv_ref[:, cols(h)], (((1,), (1,)), ((), ())),
                              preferred_element_type=F32) for h in range(X_HEADS)]
    heads = []
    for h in range(X_HEADS):
        s = scores[h] * (X_HEAD_DIM ** -0.5)
        p = jnp.exp(s - jnp.max(s, axis=-1, keepdims=True))
        den = jnp.sum(p, axis=-1, keepdims=True)
        o = jnp.dot(p.astype(BF16), kv_ref[:, cols(h, X_WIDTH)], preferred_element_type=F32)
        heads.append((o / den).astype(BF16))
    y = jnp.dot(jnp.concatenate(heads, axis=1), wo_bf16_ref[...], preferred_element_type=F32)
    hn = h_ref[...] + _rms(y, gp_ref[...])
    hn_ref[...] = hn
    a_ref[...] = _rms(hn, gn_ref[...]).astype(a_ref.dtype)


def _xattn_out(q, kv, w_o, h, g_post, g_next, layer, *, batch, seq, rows=256):
    rows = min(rows, seq)
    nt = seq // rows
    d = h.shape[1]
    row_spec = pl.BlockSpec((rows, d), lambda b, t: (b * nt + t, 0))
    gain_spec = pl.BlockSpec((None, 1, d), lambda b, t: (layer, 0, 0))
    return pl.pallas_call(
        _xattn_out_kernel,
        grid=(batch, nt),
        in_specs=[pl.BlockSpec((rows, X_WIDTH), lambda b, t: (b * nt + t, 0)),
                  pl.BlockSpec((MEM_LEN, 2 * X_WIDTH), lambda b, t: (b, 0)),
                  pl.BlockSpec((None, X_WIDTH, d), lambda b, t: (layer, 0, 0)),
                  row_spec, gain_spec, gain_spec],
        out_specs=[row_spec, row_spec],
        out_shape=[jax.ShapeDtypeStruct((batch * seq, d), F32), jax.ShapeDtypeStruct((batch * seq, d), BF16)],
        scratch_shapes=[pltpu.VMEM((X_WIDTH, d), BF16)],
        compiler_params=_params(("arbitrary", "arbitrary")),
        name="xattn_out",
    )(q, kv, w_o, h, g_post, g_next)


def _pad_lanes(v):
    return jnp.pad(v, ((0, 0), (0, LANES - v.shape[1])))[:, None, :]


def kernel(x, mem, norm_mix_pre, w_in, attn_sinks, dn_conv_w, dn_a_log, dn_dt_bias, dn_out_norm,
           w_out, norm_mix_post, norm_mem, norm_x_pre, w_xq, w_xkv, w_xo, norm_x_post,
           norm_ffn_pre, w_gate_up, w_down, norm_ffn_post):
    batch, seq, d = x.shape
    tokens = batch * seq
    h = x.reshape(tokens, d)
    memf = mem.reshape(batch * MEM_LEN, d)
    gain = lambda g: g[:, None, :]
    g_mix_pre, g_mix_post, g_mem = gain(norm_mix_pre), gain(norm_mix_post), gain(norm_mem)
    g_x_pre, g_x_post = gain(norm_x_pre), gain(norm_x_post)
    g_ffn_pre, g_ffn_post = gain(norm_ffn_pre), gain(norm_ffn_post)
    g_dn_out = gain(dn_out_norm)
    alog = _pad_lanes(dn_a_log)
    dtb = _pad_lanes(dn_dt_bias)
    w_in_t = jnp.swapaxes(w_in, 1, 2)

    a = _norm_cast(h, g_mix_pre, 0, rows=256)
    for l in range(DEPTH):
        u = _matmul_nt(a, w_in_t, l, row_off_blocks=0, n_cols=IN_MAIN_COLS, tm=2048, tn=512,
                       out_dtype=F32, name="in_proj")
        attn = _swa(u, attn_sinks, l, batch=batch, seq=seq)
        gb, gt = _dn_gates(a, w_in_t, alog, dtb, l)
        dn = _deltanet(u, gb, gt, dn_conv_w, g_dn_out, l, batch=batch, seq=seq)
        mix = _matmul_cat(attn, dn, w_out, l, tm=1024, tn=512, out_dtype=BF16, name="out_proj")
        h, qx = _resnorm_proj(mix, h, g_mix_post, g_x_pre, w_xq, l)
        mem_n = _norm_cast(memf, g_mem, l, rows=256)
        kv = _matmul(mem_n, w_xkv, l, col_off_blocks=0, n_cols=2 * X_WIDTH, tm=batch * MEM_LEN, tn=512,
                     out_dtype=BF16, name="xkv_proj")
        h, a = _xattn_out(qx, kv, w_xo, h, g_x_post, g_ffn_pre, l, batch=batch, seq=seq)
        hid = _matmul_swiglu(a, w_gate_up, l, tm=2048, tn=256)
        ff = _matmul(hid, w_down, l, col_off_blocks=0, n_cols=d, tm=1024, tn=256,
                     out_dtype=BF16, name="ffn_down")
        if l + 1 < DEPTH:
            h, a = _resnorm(ff, h, g_ffn_post, l, g_mix_pre, l + 1)
        else:
            h, _ = _resnorm(ff, h, g_ffn_post, l, None, None)
    return h.reshape(batch, seq, d)
```

```python
import functools

import jax
import jax.numpy as jnp
from jax import lax
from jax.experimental import pallas as pl
from jax.experimental.pallas import tpu as pltpu

F32 = jnp.float32
BF16 = jnp.bfloat16

D_MODEL = 4096
DEPTH = 4
MEM_LEN = 256
ATTN_WIDTH = 2048
DN_WIDTH = 2048
ATTN_HEAD_DIM = 64
ATTN_Q_HEADS = 32
ATTN_KV_HEADS = 4
ATTN_GROUP = 8
ATTN_KV_WIDTH = 256
WINDOW = 128
ATTN_BLOCK = 128
DN_HEAD_DIM = 128
DN_HEADS = 16
CONV_WIDTH = 4
DN_CHUNK = 64
X_HEADS = 4
X_HEAD_DIM = 128
X_WIDTH = 512
FFN_HIDDEN = 11008
RMS_EPS = 1e-6

OFF_AQ = 0
OFF_AK = OFF_AQ + ATTN_WIDTH
OFF_AV = OFF_AK + ATTN_KV_WIDTH
OFF_DQKV = OFF_AV + ATTN_KV_WIDTH
OFF_DZ = OFF_DQKV + 3 * DN_WIDTH
OFF_DA = OFF_DZ + DN_WIDTH
IN_MAIN_COLS = OFF_DA
N_GATE_COLS = 2 * DN_HEADS

LANES = 128
SUBLANES = 8
VMEM_LIMIT_BYTES = 56 * 1024 * 1024


def _params(semantics, vmem=VMEM_LIMIT_BYTES):
    return pltpu.CompilerParams(dimension_semantics=semantics, vmem_limit_bytes=vmem)


def _rms(x, gain):
    ms = jnp.mean(x * x, axis=-1, keepdims=True)
    return x * lax.rsqrt(ms + RMS_EPS) * gain


def _norm_kernel(x_ref, g_ref, o_ref):
    o_ref[...] = _rms(x_ref[...], g_ref[...]).astype(o_ref.dtype)


def _norm_cast(x, gains, layer, *, rows):
    m, d = x.shape
    rows = min(rows, m)
    return pl.pallas_call(
        _norm_kernel,
        grid=(m // rows,),
        in_specs=[pl.BlockSpec((rows, d), lambda i: (i, 0)),
                  pl.BlockSpec((None, 1, d), lambda i: (layer, 0, 0))],
        out_specs=pl.BlockSpec((rows, d), lambda i: (i, 0)),
        out_shape=jax.ShapeDtypeStruct((m, d), BF16),
        compiler_params=_params(("parallel",)),
        name="norm_cast",
    )(x, gains)


def _resnorm_kernel(y_ref, h_ref, gp_ref, gn_ref, hn_ref, a_ref):
    hn = h_ref[...] + _rms(y_ref[...].astype(F32), gp_ref[...])
    hn_ref[...] = hn
    a_ref[...] = _rms(hn, gn_ref[...]).astype(a_ref.dtype)


def _resnorm_proj_kernel(y_ref, h_ref, gp_ref, gn_ref, w_ref, hn_ref, q_ref, w_bf16_ref):
    @pl.when(pl.program_id(0) == 0)
    def _():
        w_bf16_ref[...] = w_ref[...].astype(BF16)

    hn = h_ref[...] + _rms(y_ref[...].astype(F32), gp_ref[...])
    hn_ref[...] = hn
    a = _rms(hn, gn_ref[...]).astype(BF16)
    q_ref[...] = jnp.dot(a, w_bf16_ref[...], preferred_element_type=F32).astype(q_ref.dtype)


def _resnorm_proj(y, h, g_post, g_next, w, layer, *, rows=256):
    m, d = y.shape
    n = w.shape[2]
    rows = min(rows, m)
    row_spec = pl.BlockSpec((rows, d), lambda i: (i, 0))
    gain_spec = pl.BlockSpec((None, 1, d), lambda i: (layer, 0, 0))
    return pl.pallas_call(
        _resnorm_proj_kernel,
        grid=(m // rows,),
        in_specs=[row_spec, row_spec, gain_spec, gain_spec,
                  pl.BlockSpec((None, d, n), lambda i: (layer, 0, 0))],
        out_specs=[row_spec, pl.BlockSpec((rows, n), lambda i: (i, 0))],
        out_shape=[jax.ShapeDtypeStruct((m, d), F32), jax.ShapeDtypeStruct((m, n), BF16)],
        scratch_shapes=[pltpu.VMEM((d, n), BF16)],
        compiler_params=_params(("arbitrary",)),
        name="resnorm_xq",
    )(y, h, g_post, g_next, w)


def _resnorm_last_kernel(y_ref, h_ref, gp_ref, hn_ref):
    hn_ref[...] = h_ref[...] + _rms(y_ref[...].astype(F32), gp_ref[...])


def _resnorm(y, h, g_post, l_post, g_next, l_next, *, rows=256):
    m, d = y.shape
    rows = min(rows, m)
    row_spec = pl.BlockSpec((rows, d), lambda i: (i, 0))
    if g_next is None:
        return pl.pallas_call(
            _resnorm_last_kernel,
            grid=(m // rows,),
            in_specs=[row_spec, row_spec, pl.BlockSpec((None, 1, d), lambda i: (l_post, 0, 0))],
            out_specs=row_spec,
            out_shape=jax.ShapeDtypeStruct((m, d), F32),
            compiler_params=_params(("parallel",)),
            name="resnorm_last",
        )(y, h, g_post), None
    return pl.pallas_call(
        _resnorm_kernel,
        grid=(m // rows,),
        in_specs=[row_spec, row_spec,
                  pl.BlockSpec((None, 1, d), lambda i: (l_post, 0, 0)),
                  pl.BlockSpec((None, 1, d), lambda i: (l_next, 0, 0))],
        out_specs=[row_spec, row_spec],
        out_shape=[jax.ShapeDtypeStruct((m, d), F32), jax.ShapeDtypeStruct((m, d), BF16)],
        compiler_params=_params(("parallel",)),
        name="resnorm",
    )(y, h, g_post, g_next)


ROW_TILE_PREFETCH_BYTES = 16 * 1024 * 1024


def _row_resident_spec(tm, k):
    if 2 * tm * k * 2 <= ROW_TILE_PREFETCH_BYTES:
        return pl.BlockSpec((tm, k), lambda i, j: (i, 0))
    return pl.BlockSpec((tm, k), lambda i, j: (i, 0), pipeline_mode=pl.Buffered(1))


def _mm_kernel(x_ref, w_ref, o_ref):
    acc = jnp.dot(x_ref[...], w_ref[...].astype(BF16), preferred_element_type=F32)
    o_ref[...] = acc.astype(o_ref.dtype)


def _matmul(x, w, layer, *, col_off_blocks, n_cols, tm, tn, out_dtype, name):
    m, k = x.shape
    tm = min(tm, m)
    assert w.shape[1] == k and m % tm == 0 and n_cols % tn == 0
    assert (col_off_blocks * tn + n_cols) <= w.shape[2]
    return pl.pallas_call(
        _mm_kernel,
        grid=(m // tm, n_cols // tn),
        in_specs=[_row_resident_spec(tm, k),
                  pl.BlockSpec((None, k, tn), lambda i, j: (layer, 0, j + col_off_blocks))],
        out_specs=pl.BlockSpec((tm, tn), lambda i, j: (i, j)),
        out_shape=jax.ShapeDtypeStruct((m, n_cols), out_dtype),
        compiler_params=_params(("parallel", "arbitrary")),
        name=name,
    )(x, w)


def _mm_nt_kernel(x_ref, wt_ref, o_ref):
    acc = lax.dot_general(x_ref[...], wt_ref[...].astype(BF16), (((1,), (1,)), ((), ())),
                          preferred_element_type=F32)
    o_ref[...] = acc.astype(o_ref.dtype)


def _matmul_nt(x, wt, layer, *, row_off_blocks, n_cols, tm, tn, out_dtype, name):
    m, k = x.shape
    tm = min(tm, m)
    assert wt.shape[2] == k and m % tm == 0 and n_cols % tn == 0
    assert (row_off_blocks * tn + n_cols) <= wt.shape[1]
    return pl.pallas_call(
        _mm_nt_kernel,
        grid=(m // tm, n_cols // tn),
        in_specs=[_row_resident_spec(tm, k),
                  pl.BlockSpec((None, tn, k), lambda i, j: (layer, j + row_off_blocks, 0))],
        out_specs=pl.BlockSpec((tm, tn), lambda i, j: (i, j)),
        out_shape=jax.ShapeDtypeStruct((m, n_cols), out_dtype),
        compiler_params=_params(("parallel", "arbitrary")),
        name=name,
    )(x, wt)


def _in_proj_kernel(x_ref, wt_ref, oa_ref, od_ref, *, attn_blocks):
    acc = lax.dot_general(x_ref[...], wt_ref[...].astype(BF16), (((1,), (1,)), ((), ())),
                          preferred_element_type=F32)
    j = pl.program_id(1)

    @pl.when(j < attn_blocks)
    def _():
        oa_ref[...] = acc

    @pl.when(j >= attn_blocks)
    def _():
        od_ref[...] = acc


def _in_proj(x, wt, layer, *, tm, tn):
    m, k = x.shape
    tm = min(tm, m)
    assert m % tm == 0 and OFF_DQKV % tn == 0 and IN_MAIN_COLS % tn == 0
    attn_blocks = OFF_DQKV // tn
    n_blocks = IN_MAIN_COLS // tn
    return pl.pallas_call(
        functools.partial(_in_proj_kernel, attn_blocks=attn_blocks),
        grid=(m // tm, n_blocks),
        in_specs=[_row_resident_spec(tm, k),
                  pl.BlockSpec((None, tn, k), lambda i, j: (layer, j, 0))],
        out_specs=[pl.BlockSpec((tm, tn), lambda i, j: (i, jnp.minimum(j, attn_blocks - 1))),
                   pl.BlockSpec((tm, tn), lambda i, j: (i, jnp.maximum(j - attn_blocks, 0)))],
        out_shape=[jax.ShapeDtypeStruct((m, OFF_DQKV), F32),
                   jax.ShapeDtypeStruct((m, IN_MAIN_COLS - OFF_DQKV), F32)],
        compiler_params=_params(("arbitrary", "arbitrary")),
        name="in_proj",
    )(x, wt)


def _mm2_kernel(x1_ref, x2_ref, w_ref, o_ref):
    k1 = x1_ref.shape[1]
    w = w_ref[...].astype(BF16)
    acc = jnp.dot(x1_ref[...], w[:k1], preferred_element_type=F32)
    acc = acc + jnp.dot(x2_ref[...], w[k1:], preferred_element_type=F32)
    o_ref[...] = acc.astype(o_ref.dtype)


def _matmul_cat(x1, x2, w, layer, *, tm, tn, out_dtype, name):
    m, k1 = x1.shape
    k2 = x2.shape[1]
    n = w.shape[2]
    tm = min(tm, m)
    assert w.shape[1] == k1 + k2 and m % tm == 0 and n % tn == 0
    return pl.pallas_call(
        _mm2_kernel,
        grid=(m // tm, n // tn),
        in_specs=[_row_resident_spec(tm, k1), _row_resident_spec(tm, k2),
                  pl.BlockSpec((None, k1 + k2, tn), lambda i, j: (layer, 0, j))],
        out_specs=pl.BlockSpec((tm, tn), lambda i, j: (i, j)),
        out_shape=jax.ShapeDtypeStruct((m, n), out_dtype),
        compiler_params=_params(("parallel", "arbitrary")),
        name=name,
    )(x1, x2, w)


def _swiglu_kernel(x_ref, wg_ref, wu_ref, o_ref):
    x = x_ref[...]
    g = jnp.dot(x, wg_ref[...].astype(BF16), preferred_element_type=F32)
    u = jnp.dot(x, wu_ref[...].astype(BF16), preferred_element_type=F32)
    o_ref[...] = (g * jax.nn.sigmoid(g) * u).astype(o_ref.dtype)


def _matmul_swiglu(x, w, layer, *, tm, tn):
    m, k = x.shape
    tm = min(tm, m)
    hid = w.shape[2] // 2
    assert hid % tn == 0 and m % tm == 0
    up_off = hid // tn
    return pl.pallas_call(
        _swiglu_kernel,
        grid=(m // tm, hid // tn),
        in_specs=[_row_resident_spec(tm, k),
                  pl.BlockSpec((None, k, tn), lambda i, j: (layer, 0, j)),
                  pl.BlockSpec((None, k, tn), lambda i, j: (layer, 0, j + up_off))],
        out_specs=pl.BlockSpec((tm, tn), lambda i, j: (i, j)),
        out_shape=jax.ShapeDtypeStruct((m, hid), BF16),
        compiler_params=_params(("parallel", "arbitrary")),
        name="ffn_gate_up",
    )(x, w, w)


SWA_LOOKAHEAD = 3


def _alibi_slope(head):
    return 2.0 ** (-8.0 * (head + 1) / ATTN_Q_HEADS)


def _block_diag_pair(blk, use_right):
    lane = lax.broadcasted_iota(jnp.int32, blk.shape, 1)
    left = lane < ATTN_HEAD_DIM
    rolled = pltpu.roll(blk, ATTN_HEAD_DIM, 1)
    if use_right:
        top, bot = jnp.where(left, rolled, 0.0), jnp.where(left, 0.0, blk)
    else:
        top, bot = jnp.where(left, blk, 0.0), jnp.where(left, 0.0, rolled)
    return jnp.concatenate([top, bot], axis=0)


def _swa_kernel(sink_ref, q_ref, kp_ref, kc_ref, vp_ref, vc_ref, o_ref, *, layer):
    n = pl.program_id(1)
    blk = ATTN_BLOCK
    kw = jnp.concatenate([kp_ref[...], kc_ref[...]], axis=0)
    vw = jnp.concatenate([vp_ref[...], vc_ref[...]], axis=0)
    r = lax.broadcasted_iota(jnp.int32, (blk, blk), 0)
    c = lax.broadcasted_iota(jnp.int32, (blk, blk), 1)
    upper = c > r
    dist = jnp.where(upper, r + blk - c, r - c).astype(F32)
    dist = jnp.where(upper & (n == 0), jnp.inf, dist)
    out_lane = lax.broadcasted_iota(jnp.int32, (blk, LANES), 1)

    kv_operands = []
    for kv_head in range(ATTN_KV_HEADS):
        lanes = slice((kv_head // 2) * LANES, (kv_head // 2 + 1) * LANES)
        kv_operands.append((_block_diag_pair(kw[:, lanes], kv_head % 2 == 1).astype(BF16),
                            _block_diag_pair(vw[:, lanes], kv_head % 2 == 1).astype(BF16)))

    def cols(head_a):
        return slice(head_a * ATTN_HEAD_DIM, (head_a + 2) * ATTN_HEAD_DIM)

    def scores(head_a):
        q2 = (q_ref[:, cols(head_a)] * (ATTN_HEAD_DIM ** -0.5)).astype(BF16)
        return lax.dot_general(q2, kv_operands[head_a // ATTN_GROUP][0], (((1,), (1,)), ((), ())),
                               preferred_element_type=F32)

    def finish(head_a, s2):
        probs, inv_den = [], []
        for t in range(2):
            head = head_a + t
            s_prev = s2[:, 2 * t * blk:(2 * t + 1) * blk]
            s_cur = s2[:, (2 * t + 1) * blk:(2 * t + 2) * blk]
            s = jnp.where(upper, s_prev, s_cur) - _alibi_slope(head) * dist
            sink = sink_ref[layer, head]
            mx = jnp.maximum(jnp.max(s, axis=-1, keepdims=True), sink)
            p = jnp.exp(s - mx)
            den = jnp.sum(p, axis=-1, keepdims=True) + jnp.exp(sink - mx)
            probs += [jnp.where(upper, p, 0.0), jnp.where(upper, 0.0, p)]
            inv_den.append(1.0 / den)
        p2 = jnp.concatenate(probs, axis=1).astype(BF16)
        o2 = jnp.dot(p2, kv_operands[head_a // ATTN_GROUP][1], preferred_element_type=F32)
        scale = jnp.where(out_lane < ATTN_HEAD_DIM, inv_den[0], inv_den[1])
        o_ref[:, cols(head_a)] = (o2 * scale).astype(o_ref.dtype)

    pair_heads = list(range(0, ATTN_Q_HEADS, 2))
    pending = [scores(hd) for hd in pair_heads[:SWA_LOOKAHEAD]]
    for i, head_a in enumerate(pair_heads):
        if i + SWA_LOOKAHEAD < len(pair_heads):
            pending.append(scores(pair_heads[i + SWA_LOOKAHEAD]))
        finish(head_a, pending.pop(0))


def _swa(u, sinks, layer, *, batch, seq):
    nb = seq // ATTN_BLOCK
    kcol = OFF_AK // ATTN_KV_WIDTH
    vcol = OFF_AV // ATTN_KV_WIDTH

    def cur(col):
        return lambda b, n: (b * nb + n, col)

    def prev(col):
        return lambda b, n: (b * nb + jnp.maximum(n - 1, 0), col)

    kv_shape = (ATTN_BLOCK, ATTN_KV_WIDTH)
    return pl.pallas_call(
        functools.partial(_swa_kernel, layer=layer),
        grid=(batch, nb),
        in_specs=[pl.BlockSpec(memory_space=pltpu.SMEM),
                  pl.BlockSpec((ATTN_BLOCK, ATTN_WIDTH), lambda b, n: (b * nb + n, 0)),
                  pl.BlockSpec(kv_shape, prev(kcol)), pl.BlockSpec(kv_shape, cur(kcol)),
                  pl.BlockSpec(kv_shape, prev(vcol)), pl.BlockSpec(kv_shape, cur(vcol))],
        out_specs=pl.BlockSpec((ATTN_BLOCK, ATTN_WIDTH), lambda b, n: (b * nb + n, 0)),
        out_shape=jax.ShapeDtypeStruct((batch * seq, ATTN_WIDTH), BF16),
        compiler_params=_params(("parallel", "parallel")),
        name="swa",
    )(sinks, u, u, u, u, u)


def _dn_gates_kernel(a_ref, wt_ref, alog_ref, bias_ref, gb_ref, gt_ref):
    raw = lax.dot_general(a_ref[...], wt_ref[...].astype(BF16), (((1,), (1,)), ((), ())),
                          preferred_element_type=F32)
    rows = raw.shape[0]
    lane = lax.broadcasted_iota(jnp.int32, raw.shape, 1)
    raw = jnp.where(lane < N_GATE_COLS, raw, 0.0)
    g = -jnp.exp(alog_ref[...]) * jax.nn.softplus(raw + bias_ref[...])
    pos = lax.broadcasted_iota(jnp.int32, raw.shape, 0) % DN_CHUNK
    acc = g
    shift = 1
    while shift < DN_CHUNK:
        acc = acc + jnp.where(pos >= shift, pltpu.roll(acc, shift, 0), 0.0)
        shift *= 2
    gb = jnp.where(lane < DN_HEADS, acc, jnp.where(lane < N_GATE_COLS, jax.nn.sigmoid(raw), 0.0))
    gb_ref[...] = gb
    gt_ref[...] = jnp.swapaxes(gb.reshape(rows // DN_CHUNK, DN_CHUNK, LANES), 1, 2)


def _dn_gates(a, w_in_t, alog, bias, layer, *, rows=512):
    m, d = a.shape
    rows = min(rows, m)
    nc = rows // DN_CHUNK
    return pl.pallas_call(
        _dn_gates_kernel,
        grid=(m // rows,),
        in_specs=[pl.BlockSpec((rows, d), lambda i: (i, 0)),
                  pl.BlockSpec((None, LANES, d), lambda i: (layer, OFF_DA // LANES, 0)),
                  pl.BlockSpec((None, 1, LANES), lambda i: (layer, 0, 0)),
                  pl.BlockSpec((None, 1, LANES), lambda i: (layer, 0, 0))],
        out_specs=[pl.BlockSpec((rows, LANES), lambda i: (i, 0)),
                   pl.BlockSpec((nc, LANES, DN_CHUNK), lambda i: (i, 0, 0))],
        out_shape=[jax.ShapeDtypeStruct((m, LANES), F32),
                   jax.ShapeDtypeStruct((m // DN_CHUNK, LANES, DN_CHUNK), F32)],
        compiler_params=_params(("parallel",)),
        name="dn_gates",
    )(a, w_in_t, alog, bias)


def _bdot(eq, a, b):
    return jnp.einsum(eq, a.astype(BF16), b.astype(BF16), preferred_element_type=F32)


def _unit_lower_inverse(low, ri, ci):
    base = SUBLANES
    same = lambda b: (ri & -b) == (ci & -b)
    x = -jnp.where(same(base), low, 0.0)
    inv = (ri == ci).astype(F32) + x
    size = 2
    while size < base:
        x = _bdot('cij,cjk->cik', x, x)
        inv = inv + _bdot('cij,cjk->cik', inv, x)
        size *= 2
    blk = base
    while blk < DN_CHUNK:
        off = jnp.where(same(2 * blk) & ~same(blk), low, 0.0)
        inv = inv - _bdot('cij,cjk->cik', inv, _bdot('cij,cjk->cik', off, inv))
        blk *= 2
    return inv


def _dn_kernel(xq_ref, xk_ref, xv_ref, z_ref, gb_ref, gt_ref, cwq_ref, cwk_ref, cwv_ref, onorm_ref,
               o_ref, state_ref, tail_ref, *, heads_per_step, rows):
    hg = pl.program_id(1)
    t = pl.program_id(2)
    nc = rows // DN_CHUNK
    dh = DN_HEAD_DIM

    @pl.when(t == 0)
    def _():
        state_ref[...] = jnp.zeros_like(state_ref)
        tail_ref[...] = jnp.zeros_like(tail_ref)

    def conv_silu(x_ref, cw_ref, idx):
        x = x_ref[...]
        ext = jnp.concatenate([tail_ref[idx], x], axis=0)
        w = cw_ref[...]
        y = x * w[CONV_WIDTH - 1:CONV_WIDTH]
        for s in range(1, CONV_WIDTH):
            y = y + pltpu.roll(ext, s, 0)[SUBLANES:] * w[CONV_WIDTH - 1 - s:CONV_WIDTH - s]
        tail_ref[idx] = x[rows - SUBLANES:]
        return y * jax.nn.sigmoid(y)

    yq = conv_silu(xq_ref, cwq_ref, 0)
    yk = conv_silu(xk_ref, cwk_ref, 1)
    yv = conv_silu(xv_ref, cwv_ref, 2)

    gball = gb_ref[...]
    lane = lax.broadcasted_iota(jnp.int32, gball.shape, 1)
    ri = lax.broadcasted_iota(jnp.int32, (DN_CHUNK, DN_CHUNK), 0)
    ci = lax.broadcasted_iota(jnp.int32, (DN_CHUNK, DN_CHUNK), 1)
    causal = ri >= ci
    strict = ri > ci

    def per_head(fn):
        return jnp.concatenate([fn(hh) for hh in range(heads_per_step)], axis=0)

    def head_tile(y, hh):
        return y[:, hh * dh:(hh + 1) * dh].reshape(nc, DN_CHUNK, dh)

    def gate_column(hh, offset):
        sel = lane == hg * heads_per_step + hh + offset
        return jnp.sum(jnp.where(sel, gball, 0.0), axis=-1, keepdims=True).reshape(nc, DN_CHUNK, 1)

    q3 = per_head(lambda hh: head_tile(yq, hh))
    k3 = per_head(lambda hh: head_tile(yk, hh))
    v3 = per_head(lambda hh: head_tile(yv, hh))
    q3 = q3 * lax.rsqrt(jnp.sum(q3 * q3, axis=-1, keepdims=True) + 1e-6) * (dh ** -0.5)
    k3 = k3 * lax.rsqrt(jnp.sum(k3 * k3, axis=-1, keepdims=True) + 1e-6)
    gc3 = per_head(lambda hh: gate_column(hh, 0))
    b3 = per_head(lambda hh: gate_column(hh, DN_HEADS))
    grow = per_head(lambda hh: gt_ref[:, pl.ds(hg * heads_per_step + hh, 1), :])
    decay = jnp.exp(jnp.where(causal, gc3 - grow, -jnp.inf))
    eg = jnp.exp(gc3)
    gl = gc3[:, DN_CHUNK - 1:DN_CHUNK, :]
    kb = k3 * b3
    kk = _bdot('cid,cjd->cij', kb, k3)
    qk = _bdot('cid,cjd->cij', q3, k3)
    low = jnp.where(strict, kk * decay, 0.0)
    a_intra = qk * decay
    tinv = _unit_lower_inverse(low, ri, ci)
    rhs = jnp.concatenate([kb * eg, v3 * b3], axis=-1)
    wu = _bdot('cij,cjd->cid', tinv, rhs)
    kd = k3 * jnp.exp(gl - gc3)
    pn = _bdot('ncd,nce->nde', kd, wu)
    aq = _bdot('cij,cjd->cid', a_intra, wu)
    qeff = q3 * eg - aq[..., :dh]
    neg_p_q = jnp.concatenate([-pn[..., :dh], qeff], axis=1).astype(BF16)
    n_mat = pn[..., dh:]
    o_loc = aq[..., dh:]
    g_last = jnp.exp(gl)

    states = [state_ref[hh] for hh in range(heads_per_step)]
    outs = [[] for _ in range(heads_per_step)]
    for cidx in range(nc):
        prods = [jnp.dot(neg_p_q[hh * nc + cidx], states[hh].astype(BF16), preferred_element_type=F32)
                 for hh in range(heads_per_step)]
        for hh in range(heads_per_step):
            b = hh * nc + cidx
            outs[hh].append(prods[hh][dh:] + o_loc[b])
            states[hh] = g_last[b] * states[hh] + prods[hh][:dh] + n_mat[b]
    for hh in range(heads_per_step):
        state_ref[hh] = states[hh]
        cols = slice(hh * dh, (hh + 1) * dh)
        o = jnp.concatenate(outs[hh], axis=0)
        z = z_ref[:, cols]
        o_ref[:, cols] = (_rms(o, onorm_ref[...]) * (z * jax.nn.sigmoid(z))).astype(o_ref.dtype)


def _deltanet(u, gb, gt, conv_w, out_norm, layer, *, batch, seq, heads_per_step=8, rows=512):
    wcols = heads_per_step * DN_HEAD_DIM
    rows = min(rows, seq)
    nt = seq // rows
    nc = rows // DN_CHUNK
    n_hg = DN_HEADS // heads_per_step
    assert DN_WIDTH % wcols == 0 and u.shape[1] == 4 * DN_WIDTH
    qoff = 0
    koff = DN_WIDTH // wcols
    voff = 2 * DN_WIDTH // wcols
    zoff = 3 * DN_WIDTH // wcols

    def xspec(off):
        return pl.BlockSpec((rows, wcols), lambda b, g, t: (b * nt + t, off + g))

    def cwspec(off):
        return pl.BlockSpec((None, CONV_WIDTH, wcols), lambda b, g, t: (layer, 0, off + g))

    return pl.pallas_call(
        functools.partial(_dn_kernel, heads_per_step=heads_per_step, rows=rows),
        grid=(batch, n_hg, nt),
        in_specs=[xspec(qoff), xspec(koff), xspec(voff), xspec(zoff),
                  pl.BlockSpec((rows, LANES), lambda b, g, t: (b * nt + t, 0)),
                  pl.BlockSpec((nc, LANES, DN_CHUNK), lambda b, g, t: (b * nt + t, 0, 0)),
                  cwspec(0), cwspec(DN_WIDTH // wcols), cwspec(2 * DN_WIDTH // wcols),
                  pl.BlockSpec((None, 1, DN_HEAD_DIM), lambda b, g, t: (layer, 0, 0))],
        out_specs=pl.BlockSpec((rows, wcols), lambda b, g, t: (b * nt + t, g)),
        out_shape=jax.ShapeDtypeStruct((batch * seq, DN_WIDTH), BF16),
        scratch_shapes=[pltpu.VMEM((heads_per_step, DN_HEAD_DIM, DN_HEAD_DIM), F32),
                        pltpu.VMEM((3, SUBLANES, wcols), F32)],
        compiler_params=_params(("parallel", "parallel", "arbitrary")),
        name="deltanet",
    )(u, u, u, u, gb, gt, conv_w, conv_w, conv_w, out_norm)


def _xattn_out_kernel(q_ref, kv_ref, wo_ref, h_ref, gp_ref, gn_ref, hn_ref, a_ref, wo_bf16_ref):
    @pl.when((pl.program_id(0) == 0) & (pl.program_id(1) == 0))
    def _():
        wo_bf16_ref[...] = wo_ref[...].astype(BF16)

    def cols(h, off=0):
        return slice(off + h * X_HEAD_DIM, off + (h + 1) * X_HEAD_DIM)

    scores = [lax.dot_general(q_ref[:, cols(h)], kv_ref[:, cols(h)], (((1,), (1,)), ((), ())),
                              preferred_element_type=F32) for h in range(X_HEADS)]
    heads = []
    for h in range(X_HEADS):
        s = scores[h] * (X_HEAD_DIM ** -0.5)
        p = jnp.exp(s - jnp.max(s, axis=-1, keepdims=True))
        den = jnp.sum(p, axis=-1, keepdims=True)
        o = jnp.dot(p.astype(BF16), kv_ref[:, cols(h, X_WIDTH)], preferred_element_type=F32)
        heads.append((o / den).astype(BF16))
    y = jnp.dot(jnp.concatenate(heads, axis=1), wo_bf16_ref[...], preferred_element_type=F32)
    hn = h_ref[...] + _rms(y, gp_ref[...])
    hn_ref[...] = hn
    a_ref[...] = _rms(hn, gn_ref[...]).astype(a_ref.dtype)


def _xattn_out(q, kv, w_o, h, g_post, g_next, layer, *, batch, seq, rows=256):
    rows = min(rows, seq)
    nt = seq // rows
    d = h.shape[1]
    row_spec = pl.BlockSpec((rows, d), lambda b, t: (b * nt + t, 0))
    gain_spec = pl.BlockSpec((None, 1, d), lambda b, t: (layer, 0, 0))
    return pl.pallas_call(
        _xattn_out_kernel,
        grid=(batch, nt),
        in_specs=[pl.BlockSpec((rows, X_WIDTH), lambda b, t: (b * nt + t, 0)),
                  pl.BlockSpec((MEM_LEN, 2 * X_WIDTH), lambda b, t: (b, 0)),
                  pl.BlockSpec((None, X_WIDTH, d), lambda b, t: (layer, 0, 0)),
                  row_spec, gain_spec, gain_spec],
        out_specs=[row_spec, row_spec],
        out_shape=[jax.ShapeDtypeStruct((batch * seq, d), F32), jax.ShapeDtypeStruct((batch * seq, d), BF16)],
        scratch_shapes=[pltpu.VMEM((X_WIDTH, d), BF16)],
        compiler_params=_params(("arbitrary", "arbitrary")),
        name="xattn_out",
    )(q, kv, w_o, h, g_post, g_next)


def _pad_lanes(v):
    return jnp.pad(v, ((0, 0), (0, LANES - v.shape[1])))[:, None, :]


def kernel(x, mem, norm_mix_pre, w_in, attn_sinks, dn_conv_w, dn_a_log, dn_dt_bias, dn_out_norm,
           w_out, norm_mix_post, norm_mem, norm_x_pre, w_xq, w_xkv, w_xo, norm_x_post,
           norm_ffn_pre, w_gate_up, w_down, norm_ffn_post):
    batch, seq, d = x.shape
    tokens = batch * seq
    h = x.reshape(tokens, d)
    memf = mem.reshape(batch * MEM_LEN, d)
    gain = lambda g: g[:, None, :]
    g_mix_pre, g_mix_post, g_mem = gain(norm_mix_pre), gain(norm_mix_post), gain(norm_mem)
    g_x_pre, g_x_post = gain(norm_x_pre), gain(norm_x_post)
    g_ffn_pre, g_ffn_post = gain(norm_ffn_pre), gain(norm_ffn_post)
    g_dn_out = gain(dn_out_norm)
    alog = _pad_lanes(dn_a_log)
    dtb = _pad_lanes(dn_dt_bias)
    w_in_t = jnp.swapaxes(w_in, 1, 2)

    a = _norm_cast(h, g_mix_pre, 0, rows=256)
    for l in range(DEPTH):
        u_attn, u_dn = _in_proj(a, w_in_t, l, tm=2048, tn=512)
        attn = _swa(u_attn, attn_sinks, l, batch=batch, seq=seq)
        gb, gt = _dn_gates(a, w_in_t, alog, dtb, l)
        dn = _deltanet(u_dn, gb, gt, dn_conv_w, g_dn_out, l, batch=batch, seq=seq)
        mix = _matmul_cat(attn, dn, w_out, l, tm=1024, tn=512, out_dtype=BF16, name="out_proj")
        h, qx = _resnorm_proj(mix, h, g_mix_post, g_x_pre, w_xq, l)
        mem_n = _norm_cast(memf, g_mem, l, rows=256)
        kv = _matmul(mem_n, w_xkv, l, col_off_blocks=0, n_cols=2 * X_WIDTH, tm=batch * MEM_LEN, tn=512,
                     out_dtype=BF16, name="xkv_proj")
        h, a = _xattn_out(qx, kv, w_xo, h, g_x_post, g_ffn_pre, l, batch=batch, seq=seq)
        hid = _matmul_swiglu(a, w_gate_up, l, tm=2048, tn=256)
        ff = _matmul(hid, w_down, l, col_off_blocks=0, n_cols=d, tm=1024, tn=256,
                     out_dtype=BF16, name="ffn_down")
        if l + 1 < DEPTH:
            h, a = _resnorm(ff, h, g_ffn_post, l, g_mix_pre, l + 1)
        else:
            h, _ = _resnorm(ff, h, g_ffn_post, l, None, None)
    return h.reshape(batch, seq, d)
```
